```python
import math
import jax, jax.numpy as jnp
from jax import lax
import numpy as np

D_MODEL = 2048
BATCH = 2
SEQ = 8192
DEPTH = 1

NH_M = 4
DK_M = 128
DV_M = 256
CONV_K = 5
NH_H = 8
DK_H = 128
DV_H = 128
CHUNK = 64
N_EXPERTS = 16
CAP_FACTOR = 2
D_FF_EXPERT = 5632
LN_EPS = 1e-5
DEEPNORM_ALPHA = (2.0 * DEPTH) ** 0.25
DEEPNORM_BETA = (8.0 * DEPTH) ** -0.25

QK_M = NH_M * DK_M
V_M = NH_M * DV_M
Q_H = NH_H * DK_H
V_H = NH_H * DV_H
SPLIT_SIZES = (2 * QK_M, V_M, V_M, 2 * NH_M, 2 * NH_M, Q_H, 2 * Q_H, V_H, V_H, D_MODEL, D_MODEL)
D_IN = 2 * QK_M + 2 * V_M + 4 * NH_M + 3 * Q_H + 2 * V_H + 2 * D_MODEL

kernel_name = "bidir_mlstm_hgrn2_gated_ec_moe_deepnorm"


def layer_norm(x, g, b):
    xf = x.astype(jnp.float32)
    mu = jnp.mean(xf, axis=-1, keepdims=True)
    var = jnp.mean(jnp.square(xf - mu), axis=-1, keepdims=True)
    return ((xf - mu) * lax.rsqrt(var + LN_EPS) * g + b).astype(x.dtype)


def head_rms_norm(x, g):
    xf = x.astype(jnp.float32)
    return xf * lax.rsqrt(jnp.mean(xf * xf, axis=-1, keepdims=True) + LN_EPS) * g.astype(jnp.float32)


def centred_dwconv(x, w):
    C = x.shape[-1]
    pad = CONV_K // 2
    return lax.conv_general_dilated(x, w[:, None, :].astype(x.dtype), window_strides=(1,),
                                    padding=[(pad, pad)], dimension_numbers=("NWC", "WIO", "NWC"),
                                    feature_group_count=C)


def _to_chunks(t):
    N, S = t.shape[:2]
    t = t.reshape((N, S // CHUNK, CHUNK) + t.shape[2:])
    return jnp.moveaxis(jnp.moveaxis(t, 1, 0), 3, 2)


def _from_chunks(t):
    t = jnp.moveaxis(jnp.moveaxis(t, 2, 3), 0, 1)
    N, nc, L = t.shape[:3]
    return t.reshape((N, nc * L) + t.shape[3:])


def _bidirectional(fn, shared, gates):
    B = shared[0].shape[0]
    args = [jnp.concatenate([t, jnp.flip(t, axis=1)], axis=0) for t in shared]
    args += [jnp.concatenate([g[:, :, 0], jnp.flip(g[:, :, 1], axis=1)], axis=0) for g in gates]
    out = fn(*args)
    return out[:B] + jnp.flip(out[B:], axis=1)


def mlstm_chunkwise(q, k, v, log_i, log_f):
    N, S, H, dk = q.shape
    dv = v.shape[-1]
    f32 = jnp.float32
    qc, kc, vc = (_to_chunks(t.astype(f32)) for t in (q, k, v))
    lic, lfc = (_to_chunks(t.astype(f32)) for t in (log_i, log_f))
    tri = jnp.tril(jnp.ones((CHUNK, CHUNK), dtype=bool))

    def step(carry, inp):
        C, n, m = carry
        qb, kb, vb, li, lf = inp
        g = jnp.cumsum(lf, axis=-1)
        D = jnp.where(tri, g[..., :, None] - g[..., None, :] + li[..., None, :], -jnp.inf)
        inter = g + m[..., None]
        m_t = jnp.maximum(jnp.max(D, axis=-1), inter)
        W = jnp.exp(D - m_t[..., None]) * jnp.einsum("nhtd,nhsd->nhts", qb, kb)
        sc = jnp.exp(inter - m_t)
        num = jnp.einsum("nhts,nhsv->nhtv", W, vb) + sc[..., None] * jnp.einsum("nhtd,nhdv->nhtv", qb, C)
        den = jnp.sum(W, axis=-1) + sc * jnp.einsum("nhtd,nhd->nht", qb, n)
        h = num / jnp.maximum(jnp.abs(den), jnp.exp(-m_t))[..., None]
        m_new = m_t[..., -1]
        wk = jnp.exp(g[..., -1:] - g + li - m_new[..., None])
        decay = jnp.exp(g[..., -1] + m - m_new)
        C = decay[..., None, None] * C + jnp.einsum("nhs,nhsd,nhsv->nhdv", wk, kb, vb)
        n = decay[..., None] * n + jnp.einsum("nhs,nhsd->nhd", wk, kb)
        return (C, n, m_new), h

    init = (jnp.zeros((N, H, dk, dv), f32), jnp.zeros((N, H, dk), f32), jnp.zeros((N, H), f32))
    _, hs = lax.scan(step, init, (qc, kc, vc, lic, lfc))
    return _from_chunks(hs)


def hgrn2_chunkwise(q, i, f):
    N, S, H, dk = q.shape
    dv = i.shape[-1]
    f32 = jnp.float32
    qc, ic, fc = (_to_chunks(t.astype(f32)) for t in (q, i, f))
    tri = jnp.tril(jnp.ones((CHUNK, CHUNK), dtype=bool))[..., None]

    def step(state, inp):
        qb, ib, fb = inp
        kb = 1.0 - fb
        b = jnp.cumsum(jnp.log(fb), axis=-2)
        dec = jnp.exp(jnp.where(tri, b[..., :, None, :] - b[..., None, :, :], -jnp.inf))
        A = jnp.einsum("nhtd,nhsd,nhtsd->nhts", qb, kb, dec)
        o = jnp.einsum("nhts,nhsv->nhtv", A, ib) + jnp.einsum("nhtd,nhdv->nhtv", qb * jnp.exp(b), state)
        b_last = b[..., -1:, :]
        state = jnp.exp(b_last[..., 0, :])[..., None] * state + jnp.einsum(
            "nhsd,nhsv->nhdv", kb * jnp.exp(b_last - b), ib)
        return state, o

    _, os_ = lax.scan(step, jnp.zeros((N, H, dk, dv), f32), (qc, ic, fc))
    return _from_chunks(os_)


def hybrid_mixer(h, w_in, b_in, conv_w, mlstm_norm_g, hgrn_norm_g, lb, w_branch_m, w_branch_h, w_out):
    B, S, _ = h.shape
    f32 = jnp.float32
    proj = jnp.einsum("bsd,de->bse", h, w_in) + b_in
    cuts = [int(c) for c in np.cumsum(SPLIT_SIZES)[:-1]]
    (qk_m, v_m, o_m, i_m, f_m, q_h, f_h, i_h, g_h, gate_m, gate_h) = jnp.split(proj, cuts, axis=-1)

    qk = jax.nn.silu(centred_dwconv(qk_m, conv_w))
    q_m = qk[..., :QK_M].reshape(B, S, NH_M, DK_M)
    k_m = qk[..., QK_M:].reshape(B, S, NH_M, DK_M) * (DK_M ** -0.5)
    v = v_m.reshape(B, S, NH_M, DV_M)
    log_i = i_m.reshape(B, S, 2, NH_M).astype(f32)
    log_f = jax.nn.log_sigmoid(f_m.reshape(B, S, 2, NH_M).astype(f32))
    hm = _bidirectional(mlstm_chunkwise, (q_m, k_m, v), (log_i, log_f))
    hm = head_rms_norm(hm, mlstm_norm_g) * jax.nn.sigmoid(o_m.reshape(B, S, NH_M, DV_M).astype(f32))
    y_m = jnp.einsum("bshv,hvd->bsd", hm.astype(h.dtype), w_branch_m.reshape(NH_M, DV_M, D_MODEL))

    lbf = lb.reshape(2, NH_H, DK_H)
    f = lbf + (1.0 - lbf) * jax.nn.sigmoid(f_h.reshape(B, S, 2, NH_H, DK_H).astype(f32))
    ho = _bidirectional(hgrn2_chunkwise,
                        (q_h.reshape(B, S, NH_H, DK_H), i_h.reshape(B, S, NH_H, DV_H)), (f,))
    ho = head_rms_norm(ho, hgrn_norm_g) * jax.nn.silu(g_h.reshape(B, S, NH_H, DV_H).astype(f32))
    y_h = jnp.einsum("bshv,hvd->bsd", ho.astype(h.dtype), w_branch_h.reshape(NH_H, DV_H, D_MODEL))

    merged = jax.nn.sigmoid(gate_m) * y_m + jax.nn.sigmoid(gate_h) * y_h
    return jnp.einsum("bsd,de->bse", merged, w_out)


def expert_choice_ffn(x, w_router, w_gate, w_up, w_down):
    B, S, D = x.shape
    cap = CAP_FACTOR * S // N_EXPERTS
    logits = jnp.einsum("bsd,de->bse", x, w_router).astype(jnp.float32)
    aff = jax.nn.softmax(logits, axis=-1)
    gate, idx = lax.top_k(jnp.swapaxes(aff, 1, 2), cap)
    bidx = jnp.arange(B)[:, None, None]
    xe = x[bidx, idx]
    hid = jax.nn.silu(jnp.einsum("becd,edf->becf", xe, w_gate)) * jnp.einsum("becd,edf->becf", xe, w_up)
    ye = jnp.einsum("becf,efd->becd", hid, w_down) * gate[..., None].astype(x.dtype)
    return jnp.zeros_like(x).at[bidx, idx].add(ye)


def setup_inputs(seed: int = 0) -> dict:
    key = jax.random.key(seed)
    ks = jax.random.split(key, 24)
    nrm = jax.random.normal
    f32 = jnp.float32
    x = nrm(ks[0], (BATCH, SEQ, D_MODEL), f32)
    ln_in_g = 1.0 + 0.02 * nrm(ks[1], (D_MODEL,), f32)
    ln_in_b = 0.02 * nrm(ks[2], (D_MODEL,), f32)
    hgrn_lb_logits = 0.5 * nrm(ks[3], (2, DEPTH + 1, Q_H), f32)
    w_in = nrm(ks[4], (DEPTH, D_MODEL, D_IN), f32) * D_MODEL ** -0.5
    f_off = 2 * QK_M + 2 * V_M + 2 * NH_M
    b_in = 0.02 * nrm(ks[5], (DEPTH, D_IN), f32)
    b_in = b_in.at[:, f_off:f_off + 2 * NH_M].add(jnp.tile(jnp.linspace(3.0, 6.0, NH_M), 2))
    conv_w = nrm(ks[6], (DEPTH, CONV_K, 2 * QK_M), f32) * CONV_K ** -0.5
    mlstm_norm_g = 1.0 + 0.02 * nrm(ks[7], (DEPTH, NH_M, DV_M), f32)
    hgrn_norm_g = 1.0 + 0.02 * nrm(ks[8], (DEPTH, NH_H, DV_H), f32)
    w_branch_m = nrm(ks[9], (DEPTH, V_M, D_MODEL), f32) * (V_M ** -0.5) * DEEPNORM_BETA
    w_branch_h = nrm(ks[10], (DEPTH, V_H, D_MODEL), f32) * (V_H ** -0.5) * DEEPNORM_BETA
    w_out = nrm(ks[11], (DEPTH, D_MODEL, D_MODEL), f32) * (D_MODEL ** -0.5) * DEEPNORM_BETA
    ln1_g = 1.0 + 0.02 * nrm(ks[12], (DEPTH, D_MODEL), f32)
    ln1_b = 0.02 * nrm(ks[13], (DEPTH, D_MODEL), f32)
    w_router = nrm(ks[14], (DEPTH, D_MODEL, N_EXPERTS), f32) * D_MODEL ** -0.5
    w_gate_e = nrm(ks[15], (DEPTH, N_EXPERTS, D_MODEL, D_FF_EXPERT), f32) * D_MODEL ** -0.5
    w_up_e = nrm(ks[16], (DEPTH, N_EXPERTS, D_MODEL, D_FF_EXPERT), f32) * D_MODEL ** -0.5
    w_down_e = nrm(ks[17], (DEPTH, N_EXPERTS, D_FF_EXPERT, D_MODEL), f32) * (D_FF_EXPERT ** -0.5) * DEEPNORM_BETA
    ln2_g = 1.0 + 0.02 * nrm(ks[18], (DEPTH, D_MODEL), f32)
    ln2_b = 0.02 * nrm(ks[19], (DEPTH, D_MODEL), f32)
    return {"x": x, "ln_in_g": ln_in_g, "ln_in_b": ln_in_b, "hgrn_lb_logits": hgrn_lb_logits,
            "w_in": w_in, "b_in": b_in, "conv_w": conv_w, "mlstm_norm_g": mlstm_norm_g,
            "hgrn_norm_g": hgrn_norm_g, "w_branch_m": w_branch_m, "w_branch_h": w_branch_h,
            "w_out": w_out, "ln1_g": ln1_g, "ln1_b": ln1_b, "w_router": w_router,
            "w_gate_e": w_gate_e, "w_up_e": w_up_e, "w_down_e": w_down_e, "ln2_g": ln2_g, "ln2_b": ln2_b}


def reference(x, ln_in_g, ln_in_b, hgrn_lb_logits, w_in, b_in, conv_w, mlstm_norm_g, hgrn_norm_g,
              w_branch_m, w_branch_h, w_out, ln1_g, ln1_b, w_router, w_gate_e, w_up_e, w_down_e,
              ln2_g, ln2_b):
    h = layer_norm(x, ln_in_g, ln_in_b)
    lb_all = jnp.cumsum(jax.nn.softmax(hgrn_lb_logits.astype(jnp.float32), axis=1), axis=1)
    for l in range(DEPTH):
        mix = hybrid_mixer(h, w_in[l], b_in[l], conv_w[l], mlstm_norm_g[l], hgrn_norm_g[l], lb_all[:, l],
                           w_branch_m[l], w_branch_h[l], w_out[l])
        h = layer_norm(DEEPNORM_ALPHA * h + mix, ln1_g[l], ln1_b[l])
        ffn = expert_choice_ffn(h, w_router[l], w_gate_e[l], w_up_e[l], w_down_e[l])
        h = layer_norm(DEEPNORM_ALPHA * h + ffn, ln2_g[l], ln2_b[l])
    return h
```

```python
import functools

import numpy as np
import jax
import jax.numpy as jnp
from jax import lax
from jax.experimental import pallas as pl
from jax.experimental.pallas import tpu as pltpu

F32 = jnp.float32
BF16 = jnp.bfloat16
I32 = jnp.int32

LN_EPS = 1e-5
CAP_FACTOR = 2
MLSTM_CHUNK = 256
HGRN_CHUNK = 64
VMEM_LIMIT = 56 * 1024 * 1024

_NT = (((1,), (1,)), ((), ()))
_TN = (((0,), (0,)), ((), ()))


def _cparams(sem):
    return pltpu.CompilerParams(dimension_semantics=sem, vmem_limit_bytes=VMEM_LIMIT)


def _dot(a, b):
    return jnp.dot(a, b, preferred_element_type=F32)


def _dot_nt(a, b):
    return lax.dot_general(a, b, _NT, preferred_element_type=F32)


def _dot_tn(a, b):
    return lax.dot_general(a, b, _TN, preferred_element_type=F32)


def _dot_exact(a, b):
    return jnp.dot(a, b, preferred_element_type=F32, precision=lax.Precision.HIGHEST)


def _sigmoid(x):
    return 1.0 / (1.0 + jnp.exp(-x))


def _layer_norm(x, g, b):
    mu = jnp.mean(x, axis=-1, keepdims=True)
    xc = x - mu
    var = jnp.mean(xc * xc, axis=-1, keepdims=True)
    return xc * lax.rsqrt(var + LN_EPS) * g + b


def _ln_in_body(x_ref, g_ref, b_ref, h_ref, hb_ref):
    h = _layer_norm(x_ref[...], g_ref[...], b_ref[...])
    h_ref[...] = h
    hb_ref[...] = h.astype(BF16)


def _ln_in(x2, g, b, tm=512):
    T, D = x2.shape
    return pl.pallas_call(
        _ln_in_body,
        grid=(T // tm,),
        in_specs=[pl.BlockSpec((tm, D), lambda i: (i, 0)),
                  pl.BlockSpec((1, D), lambda i: (0, 0)),
                  pl.BlockSpec((1, D), lambda i: (0, 0))],
        out_specs=[pl.BlockSpec((tm, D), lambda i: (i, 0)),
                   pl.BlockSpec((tm, D), lambda i: (i, 0))],
        out_shape=[jax.ShapeDtypeStruct((T, D), F32), jax.ShapeDtypeStruct((T, D), BF16)],
        compiler_params=_cparams(("arbitrary",)),
        name="ln_in",
    )(x2, g.reshape(1, D), b.reshape(1, D))


def _proj_body(x_ref, w_ref, b_ref, o_ref):
    o_ref[...] = (_dot(x_ref[...], w_ref[...]) + b_ref[...]).astype(o_ref.dtype)


def _proj(xb, w, b, out_dtype, tm, tn):
    T, K = xb.shape
    N = w.shape[1]
    return pl.pallas_call(
        _proj_body,
        grid=(N // tn, T // tm),
        in_specs=[pl.BlockSpec((tm, K), lambda j, i: (i, 0)),
                  pl.BlockSpec((K, tn), lambda j, i: (0, j)),
                  pl.BlockSpec((1, tn), lambda j, i: (0, j))],
        out_specs=pl.BlockSpec((tm, tn), lambda j, i: (i, j)),
        out_shape=jax.ShapeDtypeStruct((T, N), out_dtype),
        compiler_params=_cparams(("arbitrary", "arbitrary")),
        name="in_proj",
    )(xb, w, b.reshape(1, N))


def _conv_body(prev_ref, cur_ref, nxt_ref, w_ref, s_ref, o_ref, *, tr, ksz):
    i = pl.program_id(1)
    n = pl.num_programs(1)
    keep_prev = jnp.where(i > 0, 1.0, 0.0).astype(F32)
    keep_next = jnp.where(i < n - 1, 1.0, 0.0).astype(F32)
    xp = jnp.concatenate([prev_ref[0].astype(F32) * keep_prev,
                          cur_ref[0].astype(F32),
                          nxt_ref[0].astype(F32) * keep_next], axis=0)
    rows = tr + 16
    pad = ksz // 2
    acc = jnp.zeros((tr, xp.shape[1]), F32)
    for j in range(ksz):
        d = j - pad
        sh = xp if d == 0 else pltpu.roll(xp, (-d) % rows, 0)
        acc = acc + w_ref[j:j + 1, :] * sh[8:8 + tr, :]
    y = acc * _sigmoid(acc)
    o_ref[0] = (y * s_ref[...]).astype(o_ref.dtype)


def _qk_conv(proj3, conv_w, scale, C, tr=512):
    B, S, _ = proj3.shape
    ksz = conv_w.shape[0]
    hb = tr // 8
    nhb = S // 8
    return pl.pallas_call(
        functools.partial(_conv_body, tr=tr, ksz=ksz),
        grid=(B, S // tr),
        in_specs=[pl.BlockSpec((1, 8, C), lambda b, i: (b, jnp.maximum(i * hb - 1, 0), 0)),
                  pl.BlockSpec((1, tr, C), lambda b, i: (b, i, 0)),
                  pl.BlockSpec((1, 8, C), lambda b, i: (b, jnp.minimum((i + 1) * hb, nhb - 1), 0)),
                  pl.BlockSpec((ksz, C), lambda b, i: (0, 0)),
                  pl.BlockSpec((1, C), lambda b, i: (0, 0))],
        out_specs=pl.BlockSpec((1, tr, C), lambda b, i: (b, i, 0)),
        out_shape=jax.ShapeDtypeStruct((B, S, C), BF16),
        compiler_params=_cparams(("arbitrary", "arbitrary")),
        name="qk_conv",
    )(proj3, proj3, proj3, conv_w, scale)


def _log_sigmoid(x):
    return jnp.minimum(x, 0.0) - jnp.log1p(jnp.exp(-jnp.abs(x)))


def _mlstm_dir(qk_ref, v_ref, g_ref, gt_ref, o_ref, c_ref, n_ref, m_ref, *, d, L, nh, dk, dv):
    row = lax.broadcasted_iota(I32, (L, L), 0)
    col = lax.broadcasted_iota(I32, (L, L), 1)
    if d == 0:
        causal = col <= row
        causal_t = col >= row
        last = L - 1
    else:
        causal = col >= row
        causal_t = col <= row
        last = 0
    gcols = g_ref[0]
    grows = gt_ref[0]
    cs_col = _dot_exact(jnp.where(causal, 1.0, 0.0).astype(F32), _log_sigmoid(gcols))
    cs_row = _dot_exact(_log_sigmoid(grows), jnp.where(causal_t, 1.0, 0.0).astype(F32))
    qk = qk_ref[0]
    v_all = v_ref[0]
    for h in range(nh):
        ci = d * nh + h
        cf = 2 * nh + d * nh + h
        sidx = d * nh + h
        gc = cs_col[:, cf:cf + 1]
        gr = cs_row[cf:cf + 1, :]
        lic = gcols[:, ci:ci + 1]
        lir = grows[ci:ci + 1, :]
        mprev = m_ref[sidx][0:1, 0:1]
        dm = jnp.where(causal, gc - gr + lir, -jnp.inf)
        inter = gc + mprev
        mt = jnp.maximum(jnp.max(dm, axis=1, keepdims=True), inter)
        q = qk[:, h * dk:(h + 1) * dk]
        k = qk[:, (nh + h) * dk:(nh + h + 1) * dk]
        v = v_all[:, h * dv:(h + 1) * dv]
        w = jnp.exp(dm - mt) * _dot_nt(q, k)
        sc = jnp.exp(inter - mt)
        cst = c_ref[sidx]
        nst = n_ref[sidx][0:1, :]
        num = _dot(w.astype(BF16), v) + sc * _dot(q, cst.astype(BF16))
        den = jnp.sum(w, axis=1, keepdims=True) + sc * jnp.sum(q.astype(F32) * nst, axis=1, keepdims=True)
        o_ref[0, :, h * dv:(h + 1) * dv] = num / jnp.maximum(jnp.abs(den), jnp.exp(-mt))
        mnew = mt[last:last + 1, :]
        gl = gc[last:last + 1, :]
        wk = jnp.exp(gl - gc + lic - mnew)
        decay = jnp.exp(gl + mprev - mnew)
        kw = k.astype(F32) * wk
        c_ref[sidx] = decay * cst + _dot_tn(kw.astype(BF16), v)
        n_ref[sidx] = jnp.broadcast_to(decay * nst + jnp.sum(kw, axis=0, keepdims=True), (8, dk))
        m_ref[sidx] = jnp.broadcast_to(mnew, (8, 128))


def _mlstm_body(qk_f, v_f, g_f, gt_f, qk_b, v_b, g_b, gt_b, of_ref, ob_ref, c_ref, n_ref, m_ref,
                *, L, nh, dk, dv):
    @pl.when(pl.program_id(1) == 0)
    def _():
        c_ref[...] = jnp.zeros_like(c_ref)
        n_ref[...] = jnp.zeros_like(n_ref)
        m_ref[...] = jnp.zeros_like(m_ref)

    kw = dict(L=L, nh=nh, dk=dk, dv=dv)
    _mlstm_dir(qk_f, v_f, g_f, gt_f, of_ref, c_ref, n_ref, m_ref, d=0, **kw)
    _mlstm_dir(qk_b, v_b, g_b, gt_b, ob_ref, c_ref, n_ref, m_ref, d=1, **kw)


def _mlstm(qk, proj3, v_blk, gates, gates_t, nh, dk, dv):
    B, S, _ = qk.shape
    L = MLSTM_CHUNK
    nc = S // L
    G = gates.shape[-1]
    fw = lambda b, j: (b, j, 0)
    bw = lambda b, j: (b, nc - 1 - j, 0)
    specs = []
    for im in (fw, bw):
        specs += [pl.BlockSpec((1, L, 2 * nh * dk), im),
                  pl.BlockSpec((1, L, nh * dv), (lambda b, j, im=im: (im(b, j)[0], im(b, j)[1], v_blk))),
                  pl.BlockSpec((1, L, G), im),
                  pl.BlockSpec((1, G, L), (lambda b, j, im=im: (im(b, j)[0], 0, im(b, j)[1])))]
    return pl.pallas_call(
        functools.partial(_mlstm_body, L=L, nh=nh, dk=dk, dv=dv),
        grid=(B, nc),
        in_specs=specs,
        out_specs=[pl.BlockSpec((1, L, nh * dv), fw), pl.BlockSpec((1, L, nh * dv), bw)],
        out_shape=[jax.ShapeDtypeStruct((B, S, nh * dv), F32)] * 2,
        scratch_shapes=[pltpu.VMEM((2 * nh, dk, dv), F32),
                        pltpu.VMEM((2 * nh, 8, dk), F32),
                        pltpu.VMEM((2 * nh, 8, 128), F32)],
        compiler_params=_cparams(("arbitrary", "arbitrary")),
        name="mlstm_scan",
    )(qk, proj3, gates, gates_t, qk, proj3, gates, gates_t)


def _hgrn_levels(L):
    c = L // 2
    out = []
    while c >= 1:
        out.append(c)
        c //= 2
    return out


def _hgrn_sum_matrix(L, d):
    t = np.arange(L)[:, None]
    r = np.arange(L)[None, :]
    if d == 0:
        blocks = [(r <= t), (r > t)]
    else:
        blocks = [(r >= t), (r < t)]
    for c in _hgrn_levels(L):
        base = (t // (2 * c)) * (2 * c)
        mid = base + c
        second = (t - base) >= c
        if d == 0:
            m = np.where(second, (r >= mid) & (r <= t), (r > t) & (r < mid))
        else:
            m = np.where(second, (r >= mid) & (r < t), (r >= t) & (r < mid))
        blocks.append(m)
    p = np.concatenate(blocks, axis=0).astype(np.float32)
    return np.concatenate([p, p, p], axis=1)


def _hgrn_dir(q_ref, i_ref, f_ref, lb_ref, p_ref, o_ref, st_ref, *, d, L, nh, dk, dv, layer):
    levels = _hgrn_levels(L)
    rowi = lax.broadcasted_iota(I32, (L, 1), 0)
    row2 = lax.broadcasted_iota(I32, (L, L), 0)
    col2 = lax.broadcasted_iota(I32, (L, L), 1)
    slots = [lb_ref[k][d:d + 1, :] for k in range(lb_ref.shape[0])]
    top = functools.reduce(jnp.maximum, slots)
    es = [jnp.exp(s - top) for s in slots]
    lb = sum(es[:layer + 1]) / sum(es)
    f = lb + (1.0 - lb) * _sigmoid(f_ref[0])
    lgf = jnp.log(f)
    kb = 1.0 - f
    hi = lgf.astype(BF16)
    r1 = lgf - hi.astype(F32)
    mid = r1.astype(BF16)
    lo = (r1 - mid.astype(F32)).astype(BF16)
    sums = _dot(p_ref[...], jnp.concatenate([hi, mid, lo], axis=0))
    ex = jnp.exp(sums)
    eb = ex[0:L]
    ea = ex[L:2 * L]
    last = L - 1 if d == 0 else 0
    ebl = eb[last:last + 1, :]
    qf = q_ref[0].astype(F32)
    iv = i_ref[0]
    qe = (qf * eb).astype(BF16)
    ke = (kb * ea).astype(BF16)
    diag = qf * kb
    qls, kls, same = [], [], []
    for li, c in enumerate(levels):
        sh = c.bit_length() - 1
        second = ((rowi >> sh) & 1) == 1
        is_q = second if d == 0 else jnp.logical_not(second)
        el = ex[(2 + li) * L:(3 + li) * L]
        qls.append(jnp.where(is_q, qf * el, 0.0).astype(BF16))
        kls.append(jnp.where(is_q, 0.0, kb * el).astype(BF16))
        same.append((row2 >> (sh + 1)) == (col2 >> (sh + 1)))
    for h in range(nh):
        ks = slice(h * dk, (h + 1) * dk)
        vs = slice(h * dv, (h + 1) * dv)
        a = jnp.zeros((L, L), F32)
        for li in range(len(levels)):
            a = a + jnp.where(same[li], _dot_nt(qls[li][:, ks], kls[li][:, ks]), 0.0)
        ih = iv[:, vs]
        st = st_ref[d * nh + h]
        o = (_dot(a.astype(BF16), ih) + _dot_nt(qe[:, ks], st.astype(BF16))
             + jnp.sum(diag[:, ks], axis=1, keepdims=True) * ih.astype(F32))
        o_ref[0, :, vs] = o
        st_ref[d * nh + h] = st * ebl[:, ks] + _dot_tn(ih, ke[:, ks])


def _hgrn_body(q_f, i_f, f_f, q_b, i_b, f_b, lb_ref, pf_ref, pb_ref, of_ref, ob_ref, st_ref,
               *, L, nh, dk, dv, layer):
    @pl.when(pl.program_id(1) == 0)
    def _():
        st_ref[...] = jnp.zeros_like(st_ref)

    kw = dict(L=L, nh=nh, dk=dk, dv=dv, layer=layer)
    _hgrn_dir(q_f, i_f, f_f, lb_ref, pf_ref, of_ref, st_ref, d=0, **kw)
    _hgrn_dir(q_b, i_b, f_b, lb_ref, pb_ref, ob_ref, st_ref, d=1, **kw)


def _hgrn(proj3, q_blk, i_blk, fpre, lb, layer, nh, dk, dv):
    B, S, _ = proj3.shape
    L = HGRN_CHUNK
    nc = S // L
    W = nh * dk
    pf = jnp.asarray(_hgrn_sum_matrix(L, 0), BF16)
    pb = jnp.asarray(_hgrn_sum_matrix(L, 1), BF16)
    fw = lambda b, j: (b, j)
    bw = lambda b, j: (b, nc - 1 - j)
    specs = []
    for d, im in enumerate((fw, bw)):
        specs += [pl.BlockSpec((1, L, W), (lambda b, j, im=im: im(b, j) + (q_blk,))),
                  pl.BlockSpec((1, L, nh * dv), (lambda b, j, im=im: im(b, j) + (i_blk,))),
                  pl.BlockSpec((1, L, W), (lambda b, j, im=im, d=d: im(b, j) + (d,)))]
    specs += [pl.BlockSpec(lb.shape, lambda b, j: (0, 0, 0)),
              pl.BlockSpec(pf.shape, lambda b, j: (0, 0)),
              pl.BlockSpec(pb.shape, lambda b, j: (0, 0))]
    return pl.pallas_call(
        functools.partial(_hgrn_body, L=L, nh=nh, dk=dk, dv=dv, layer=layer),
        grid=(B, nc),
        in_specs=specs,
        out_specs=[pl.BlockSpec((1, L, nh * dv), lambda b, j: (b, j, 0)),
                   pl.BlockSpec((1, L, nh * dv), lambda b, j: (b, nc - 1 - j, 0))],
        out_shape=[jax.ShapeDtypeStruct((B, S, nh * dv), F32)] * 2,
        scratch_shapes=[pltpu.VMEM((2 * nh, dv, dk), F32)],
        compiler_params=_cparams(("arbitrary", "arbitrary")),
        name="hgrn2_scan",
    )(proj3, proj3, fpre, proj3, proj3, fpre, lb, pf, pb)


def _head_rms(x, nh, dh):
    outs = []
    for h in range(nh):
        xh = x[:, h * dh:(h + 1) * dh]
        outs.append(xh * lax.rsqrt(jnp.mean(xh * xh, axis=-1, keepdims=True) + LN_EPS))
    return jnp.concatenate(outs, axis=1)


def _post_body(hmf, hmb, hof, hob, om, gh, gm, ghh, h_ref, mg, hg, wbm, wbh, wout, l1g, l1b, wr,
               h1_ref, aff_ref, *, nh_m, dv_m, nh_h, dv_h, alpha):
    hm = _head_rms(hmf[...] + hmb[...], nh_m, dv_m) * mg[...] * _sigmoid(om[...].astype(F32))
    hgate = gh[...].astype(F32)
    ho = _head_rms(hof[...] + hob[...], nh_h, dv_h) * hg[...] * (hgate * _sigmoid(hgate))
    y_m = _dot(hm.astype(BF16), wbm[...])
    y_h = _dot(ho.astype(BF16), wbh[...])
    merged = _sigmoid(gm[...].astype(F32)) * y_m + _sigmoid(ghh[...].astype(F32)) * y_h
    mix = _dot(merged.astype(BF16), wout[...])
    h1 = _layer_norm(alpha * h_ref[...] + mix, l1g[...], l1b[...])
    h1_ref[...] = h1
    logits = _dot_nt(wr[...], h1.astype(BF16))
    z = jnp.exp(logits - jnp.max(logits, axis=0, keepdims=True))
    aff_ref[0] = z / jnp.sum(z, axis=0, keepdims=True)


def _post_mixer(hm_f, hm_b, ho_f, ho_b, proj, cols, h, mg, hg, wbm, wbh, wout, l1g, l1b, wr_t,
                B, S, dims, alpha, tm=256):
    T, D = h.shape
    nh_m, dv_m, nh_h, dv_h = dims
    vm, vh = nh_m * dv_m, nh_h * dv_h
    E = wr_t.shape[0]
    nbs = S // tm
    row = lambda i: (i, 0)
    const = lambda i: (0, 0)
    in_specs = [pl.BlockSpec((tm, vm), row), pl.BlockSpec((tm, vm), row),
                pl.BlockSpec((tm, vh), row), pl.BlockSpec((tm, vh), row),
                pl.BlockSpec((tm, vm), lambda i: (i, cols["o_m"] // vm)),
                pl.BlockSpec((tm, vh), lambda i: (i, cols["g_h"] // vh)),
                pl.BlockSpec((tm, D), lambda i: (i, cols["gate_m"] // D)),
                pl.BlockSpec((tm, D), lambda i: (i, cols["gate_h"] // D)),
                pl.BlockSpec((tm, D), row),
                pl.BlockSpec((1, vm), const), pl.BlockSpec((1, vh), const),
                pl.BlockSpec((vm, D), const), pl.BlockSpec((vh, D), const), pl.BlockSpec((D, D), const),
                pl.BlockSpec((1, D), const), pl.BlockSpec((1, D), const),
                pl.BlockSpec((E, D), const)]
    return pl.pallas_call(
        functools.partial(_post_body, nh_m=nh_m, dv_m=dv_m, nh_h=nh_h, dv_h=dv_h, alpha=alpha),
        grid=(T // tm,),
        in_specs=in_specs,
        out_specs=[pl.BlockSpec((tm, D), row),
                   pl.BlockSpec((1, E, tm), lambda i: (i // nbs, 0, i % nbs))],
        out_shape=[jax.ShapeDtypeStruct((T, D), F32), jax.ShapeDtypeStruct((B, E, S), F32)],
        compiler_params=_cparams(("arbitrary",)),
        name="mixer_out_ln1_router",
    )(hm_f, hm_b, ho_f, ho_b, proj, proj, proj, proj, h, mg, hg, wbm, wbh, wout, l1g, l1b, wr_t)


def _prefix_counts(mask2, upper, ones, bdl):
    mb = mask2.astype(BF16)
    within = _dot(mb, upper)
    rowtot = _dot(mb, ones)
    before = _dot(bdl, rowtot.astype(BF16))
    return within + before, within


def _topk_body(a_ref, upper_ref, ones_ref, bdl_ref, idx_ref, pos_ref, off_ref, *, E, R, cap):
    a3 = a_ref[0]

    def count(m):
        return jnp.sum(jnp.sum(jnp.where(m, 1.0, 0.0), axis=2, keepdims=True), axis=1, keepdims=True)

    def as_float(bits):
        return pltpu.bitcast(jnp.broadcast_to(bits, a3.shape), F32)

    def step(it, cur):
        cand = cur | (jnp.int32(1) << (30 - it))
        return jnp.where(count(a3 >= as_float(cand)) >= cap, cand, cur)

    thr = lax.fori_loop(0, 31, step, jnp.zeros((E, 1, 1), I32))
    gt = a3 >= as_float(thr + 1)
    eq = jnp.logical_and(a3 >= as_float(thr), jnp.logical_not(gt))
    need = cap - count(gt)
    upper, ones, bdl = upper_ref[...], ones_ref[...], bdl_ref[...]
    eq_incl, _ = _prefix_counts(jnp.where(eq, 1.0, 0.0).reshape(E * R, 128), upper, ones, bdl)
    sel = jnp.logical_or(gt, jnp.logical_and(eq, eq_incl.reshape(E, R, 128) <= need))
    self32 = jnp.where(sel, 1.0, 0.0).reshape(E * R, 128)
    incl, within = _prefix_counts(self32, upper, ones, bdl)
    pos_ref[0] = jnp.where(sel, incl.reshape(E, R, 128) - 1.0, -1.0).astype(I32)

    slot = lax.broadcasted_iota(I32, (1, cap), 1).astype(F32)
    rsub = lax.broadcasted_iota(I32, (R, cap), 0).astype(F32)
    for e in range(E):
        inc_e = incl[e * R:(e + 1) * R]
        row_incl = inc_e[:, 127:128]
        row_excl = row_incl - within[e * R:(e + 1) * R][:, 127:128]
        off_ref[0, e] = jnp.broadcast_to(row_excl, (R, 128)).astype(I32)
        ridx = jnp.sum(jnp.where(row_incl <= slot, 1.0, 0.0), axis=0, keepdims=True)
        onehot = rsub == ridx
        start = jnp.sum(jnp.where(onehot, row_excl, 0.0), axis=0, keepdims=True)
        local = slot - start
        pg = _dot_tn(within[e * R:(e + 1) * R].astype(BF16), jnp.where(onehot, 1.0, 0.0).astype(BF16))
        lane = jnp.sum(jnp.where(pg <= local, 1.0, 0.0), axis=0, keepdims=True)
        idx_ref[0, e:e + 1, :] = (ridx * 128.0 + lane).astype(I32)


def _topk(aff_t, cap):
    B, E, S = aff_t.shape
    R = S // 128
    k = np.arange(128)
    upper = jnp.asarray((k[:, None] <= k[None, :]).astype(np.float32), BF16)
    ones = jnp.ones((128, 128), BF16)
    r = np.arange(E * R)
    bdl = jnp.asarray(((r[:, None] // R == r[None, :] // R) & (r[None, :] < r[:, None])).astype(np.float32), BF16)
    const2 = lambda b: (0, 0)
    return pl.pallas_call(
        functools.partial(_topk_body, E=E, R=R, cap=cap),
        grid=(B,),
        in_specs=[pl.BlockSpec((1, E, R, 128), lambda b: (b, 0, 0, 0)),
                  pl.BlockSpec((128, 128), const2), pl.BlockSpec((128, 128), const2),
                  pl.BlockSpec((E * R, E * R), const2)],
        out_specs=[pl.BlockSpec((1, E, cap), lambda b: (b, 0, 0)),
                   pl.BlockSpec((1, E, R, 128), lambda b: (b, 0, 0, 0)),
                   pl.BlockSpec((1, E, R, 128), lambda b: (b, 0, 0, 0))],
        out_shape=[jax.ShapeDtypeStruct((B, E, cap), I32),
                   jax.ShapeDtypeStruct((B, E, R, 128), I32),
                   jax.ShapeDtypeStruct((B, E, R, 128), I32)],
        compiler_params=_cparams(("arbitrary",)),
        name="expert_topk",
    )(aff_t.reshape(B, E, R, 128), upper, ones, bdl)


def _ffn_body(idx_ref, h_hbm, wg_ref, wu_ref, wd_ref, ye_ref, xg, xb, acc, sem, *, E, B, C, rps):
    e = pl.program_id(0)
    b = pl.program_id(1)
    f = pl.program_id(2)
    nf = pl.num_programs(2)
    nrows = xg.shape[1]
    pair = e * B + b
    slot = pair % 2
    nxt = (pair + 1) % (E * B)
    e_n = nxt // B
    b_n = nxt % B

    def row_copy(bb, src_row, sl, dst_row):
        return pltpu.make_async_copy(h_hbm.at[bb, pl.ds(src_row, 1), :],
                                     xg.at[sl, pl.ds(dst_row, 1), :], sem.at[sl])

    def src_row(bb, ee, s):
        return idx_ref[(bb * E + ee) * C + jnp.minimum(s, C - 1)]

    def wait_all(sl):
        def wait(s, carry):
            row_copy(0, 0, sl, s).wait()
            return carry
        lax.fori_loop(0, nrows, wait, 0, unroll=rps)

    @pl.when(f == 0)
    def _():
        @pl.when(pair == 0)
        def _():
            def issue(s, carry):
                row_copy(b, src_row(b, e, s), slot, s).start()
                return carry
            lax.fori_loop(0, nrows, issue, 0)

        wait_all(slot)
        xb[...] = xg[slot, 0:C, :].astype(BF16)
        acc[...] = jnp.zeros_like(acc)

    for r in range(rps):
        s = f * rps + r
        row_copy(b_n, src_row(b_n, e_n, s), 1 - slot, s).start()

    x = xb[...]
    g = _dot(x, wg_ref[0].astype(BF16))
    u = _dot(x, wu_ref[0].astype(BF16))
    hid = (g * _sigmoid(g) * u).astype(BF16)
    acc[...] += _dot(hid, wd_ref[0].astype(BF16))

    @pl.when(f == nf - 1)
    def _():
        ye_ref[0, 0] = acc[...].astype(ye_ref.dtype)

        @pl.when(pair == E * B - 1)
        def _():
            wait_all(1 - slot)


def _expert_ffn(idx, h1, w_gate, w_up, w_down, tf=256):
    B, S, D = h1.shape
    E, _, F = w_gate.shape
    C = idx.shape[-1]
    nf = F // tf
    rps = pl.cdiv(C, nf)
    grid_spec = pltpu.PrefetchScalarGridSpec(
        num_scalar_prefetch=1,
        grid=(E, B, nf),
        in_specs=[pl.BlockSpec(memory_space=pl.ANY),
                  pl.BlockSpec((1, D, tf), lambda e, b, f, idx: (e, 0, f)),
                  pl.BlockSpec((1, D, tf), lambda e, b, f, idx: (e, 0, f)),
                  pl.BlockSpec((1, tf, D), lambda e, b, f, idx: (e, f, 0))],
        out_specs=pl.BlockSpec((1, 1, C, D), lambda e, b, f, idx: (b, e, 0, 0)),
        scratch_shapes=[pltpu.VMEM((2, nf * rps, D), F32), pltpu.VMEM((C, D), BF16),
                        pltpu.VMEM((C, D), F32), pltpu.SemaphoreType.DMA((2,))],
    )
    return pl.pallas_call(
        functools.partial(_ffn_body, E=E, B=B, C=C, rps=rps),
        grid_spec=grid_spec,
        out_shape=jax.ShapeDtypeStruct((B, E, C, D), BF16),
        compiler_params=_cparams(("arbitrary", "arbitrary", "arbitrary")),
        name="expert_ffn",
    )(idx.reshape(-1), h1, w_gate, w_up, w_down)


COMBINE_ROWS = 16
COMBINE_KT = 256


def _combine_body(st_ref, h_ref, pos_ref, aff_ref, ye_hbm, g_ref, b_ref, o_ref, buf, acc, sem,
                  *, E, TB, NTB, alpha):
    b = pl.program_id(0)
    tb = pl.program_id(1)
    rows, kt = COMBINE_ROWS, COMBINE_KT

    @pl.when(jnp.logical_and(b == 0, tb == 0))
    def _():
        buf[...] = jnp.zeros_like(buf)

    def chunk_copy(e, src, dst):
        return pltpu.make_async_copy(ye_hbm.at[b, e, pl.ds(src, rows), :], buf.at[pl.ds(dst, rows), :], sem)

    pos = pos_ref[0]
    aff = aff_ref[0]
    koff = jnp.int32(0)
    keys = []
    for e in range(E):
        base = (b * E + e) * (NTB + 1) + tb
        ws = (st_ref[base] // rows) * rows
        n = (st_ref[base + 1] - ws + rows - 1) // rows

        def issue(c, carry, e=e, ws=ws, koff=koff):
            chunk_copy(e, pl.multiple_of(ws + c * rows, rows), pl.multiple_of(koff + c * rows, rows)).start()
            return carry

        lax.fori_loop(0, n, issue, 0)
        pe = pos[:, e:e + 1]
        keys.append(jnp.where(pe >= 0, pe + (koff - ws), -1))
        koff = koff + n * rows

    def wait(c, carry):
        chunk_copy(0, 0, 0).wait()
        return carry

    lax.fori_loop(0, koff // rows, wait, 0)
    acc[...] = jnp.zeros_like(acc)

    def kstep(kc, carry):
        kbase = pl.multiple_of(kc * kt, kt)
        lane = lax.broadcasted_iota(I32, (TB, kt), 1) + kbase
        o = jnp.zeros((TB, kt), F32)
        for e in range(E):
            o = jnp.where(lane == keys[e], aff[:, e:e + 1], o)
        acc[...] += _dot(o.astype(BF16), buf[pl.ds(kbase, kt), :])
        return carry

    lax.fori_loop(0, (koff + kt - 1) // kt, kstep, 0)
    o_ref[0] = _layer_norm(alpha * h_ref[0] + acc[...], g_ref[...], b_ref[...])


def _combine(starts, h1, pos, aff, ye, g, bb, alpha, TB):
    B, S, D = h1.shape
    E = ye.shape[1]
    NTB = S // TB
    kmax = pl.cdiv(E * (TB + 2 * COMBINE_ROWS), COMBINE_KT) * COMBINE_KT
    grid_spec = pltpu.PrefetchScalarGridSpec(
        num_scalar_prefetch=1,
        grid=(B, NTB),
        in_specs=[pl.BlockSpec((1, TB, D), lambda b, t, st: (b, t, 0)),
                  pl.BlockSpec((1, TB, E), lambda b, t, st: (b, t, 0)),
                  pl.BlockSpec((1, TB, E), lambda b, t, st: (b, t, 0)),
                  pl.BlockSpec(memory_space=pl.ANY),
                  pl.BlockSpec((1, D), lambda b, t, st: (0, 0)),
                  pl.BlockSpec((1, D), lambda b, t, st: (0, 0))],
        out_specs=pl.BlockSpec((1, TB, D), lambda b, t, st: (b, t, 0)),
        scratch_shapes=[pltpu.VMEM((kmax, D), BF16), pltpu.VMEM((TB, D), F32),
                        pltpu.SemaphoreType.DMA(())],
    )
    return pl.pallas_call(
        functools.partial(_combine_body, E=E, TB=TB, NTB=NTB, alpha=alpha),
        grid_spec=grid_spec,
        out_shape=jax.ShapeDtypeStruct((B, S, D), F32),
        compiler_params=_cparams(("arbitrary", "arbitrary")),
        name="combine_ln2",
    )(starts.reshape(-1), h1, pos, aff, ye, g.reshape(1, D), bb.reshape(1, D))


def _layer(h, hb, B, S, lb, layer, w_in, b_in, conv_w, mg, hg, w_bm, w_bh, w_out, l1g, l1b, w_router,
           w_gate, w_up, w_down, l2g, l2b, alpha):
    T, D = h.shape
    nh_m, dv_m = mg.shape
    nh_h, dv_h = hg.shape
    ksz, qk2 = conv_w.shape
    dk_m = qk2 // (2 * nh_m)
    v_m = nh_m * dv_m
    v_h = nh_h * dv_h
    q_h = lb.shape[-1]
    dk_h = q_h // nh_h
    E = w_router.shape[1]
    ng = 4 * nh_m
    sizes = (qk2, v_m, v_m, ng // 2, ng // 2, q_h, 2 * q_h, v_h, v_h, D, D)
    names = ("qk_m", "v_m", "o_m", "i_m", "f_m", "q_h", "f_h", "i_h", "g_h", "gate_m", "gate_h")
    start = dict(zip(names, np.concatenate([[0], np.cumsum(sizes)[:-1]]).tolist()))
    size = dict(zip(names, sizes))
    sl = lambda a, n: a[..., start[n]:start[n] + size[n]]
    a_names = ("qk_m", "v_m", "o_m", "q_h", "i_h", "g_h", "gate_m", "gate_h")
    w_a = jnp.concatenate([sl(w_in, n) for n in a_names], axis=1).astype(BF16)
    b_a = jnp.concatenate([sl(b_in, n) for n in a_names])
    cols = dict(zip(a_names, np.concatenate([[0], np.cumsum([size[n] for n in a_names])[:-1]]).tolist()))
    w_f = sl(w_in, "f_h").astype(BF16)
    b_f = sl(b_in, "f_h")
    gpad = 128 - ng
    w_g = jnp.pad(jnp.concatenate([sl(w_in, "i_m"), sl(w_in, "f_m")], axis=1), ((0, 0), (0, gpad))).astype(BF16)
    b_g = jnp.pad(jnp.concatenate([sl(b_in, "i_m"), sl(b_in, "f_m")]), (0, gpad))

    proj = _proj(hb, w_a, b_a, BF16, tm=1024, tn=1024)
    fpre = _proj(hb, w_f, b_f, F32, tm=1024, tn=1024)
    gates = _proj(hb, w_g, b_g, F32, tm=1024, tn=128)[:, :ng].reshape(B, S, ng)
    proj3 = proj.reshape(B, S, -1)

    kscale = jnp.concatenate([jnp.ones((qk2 // 2,), F32), jnp.full((qk2 // 2,), dk_m ** -0.5, F32)])
    qk = _qk_conv(proj3, conv_w, kscale.reshape(1, qk2), qk2)
    hm_f, hm_b = _mlstm(qk, proj3, cols["v_m"] // v_m, gates, jnp.swapaxes(gates, 1, 2), nh_m, dk_m, dv_m)

    ho_f, ho_b = _hgrn(proj3, cols["q_h"] // q_h, cols["i_h"] // v_h, fpre.reshape(B, S, 2 * q_h), lb,
                       layer, nh_h, dk_h, dv_h)

    h1, aff_t = _post_mixer(hm_f.reshape(T, v_m), hm_b.reshape(T, v_m), ho_f.reshape(T, v_h),
                            ho_b.reshape(T, v_h), proj, cols, h, mg.reshape(1, v_m), hg.reshape(1, v_h),
                            w_bm.astype(BF16), w_bh.astype(BF16), w_out.astype(BF16),
                            l1g.reshape(1, D), l1b.reshape(1, D), w_router.T.astype(BF16),
                            B, S, (nh_m, dv_m, nh_h, dv_h), alpha)

    cap = CAP_FACTOR * S // E
    idx, pos, off = _topk(aff_t, cap)
    h1_3 = h1.reshape(B, S, D)
    ye = _expert_ffn(idx, h1_3, w_gate, w_up, w_down)
    TB = 256
    starts = jnp.concatenate([off[:, :, ::TB // 128, 0], jnp.full((B, E, 1), cap, I32)], axis=2)
    pos_t = jnp.swapaxes(pos.reshape(B, E, S), 1, 2)
    aff = jnp.swapaxes(aff_t, 1, 2)
    out = _combine(starts, h1_3, pos_t, aff, ye, l2g, l2b, alpha, TB=TB)
    return out.reshape(T, D)


def kernel(x, ln_in_g, ln_in_b, hgrn_lb_logits, w_in, b_in, conv_w, mlstm_norm_g, hgrn_norm_g,
           w_branch_m, w_branch_h, w_out, ln1_g, ln1_b, w_router, w_gate_e, w_up_e, w_down_e,
           ln2_g, ln2_b):
    B, S, D = x.shape
    depth = w_in.shape[0]
    alpha = (2.0 * depth) ** 0.25
    h, hb = _ln_in(x.reshape(B * S, D), ln_in_g, ln_in_b)
    lb_logits = jnp.swapaxes(hgrn_lb_logits.astype(F32), 0, 1)
    for l in range(depth):
        h = _layer(h, hb, B, S, lb_logits, l, w_in[l], b_in[l], conv_w[l], mlstm_norm_g[l],
                   hgrn_norm_g[l], w_branch_m[l], w_branch_h[l], w_out[l], ln1_g[l], ln1_b[l],
                   w_router[l], w_gate_e[l], w_up_e[l], w_down_e[l], ln2_g[l], ln2_b[l], alpha)
        if l + 1 < depth:
            hb = h.astype(BF16)
    return h.reshape(B, S, D)
```

```python
import functools

import numpy as np
import jax
import jax.numpy as jnp
from jax import lax
from jax.experimental import pallas as pl
from jax.experimental.pallas import tpu as pltpu

F32 = jnp.float32
BF16 = jnp.bfloat16
I32 = jnp.int32

LN_EPS = 1e-5
CAP_FACTOR = 2
MLSTM_CHUNK = 256
HGRN_CHUNK = 64
VMEM_LIMIT = 56 * 1024 * 1024

_NT = (((1,), (1,)), ((), ()))
_TN = (((0,), (0,)), ((), ()))


def _cparams(sem):
    return pltpu.CompilerParams(dimension_semantics=sem, vmem_limit_bytes=VMEM_LIMIT)


def _dot(a, b):
    return jnp.dot(a, b, preferred_element_type=F32)


def _dot_nt(a, b):
    return lax.dot_general(a, b, _NT, preferred_element_type=F32)


def _dot_tn(a, b):
    return lax.dot_general(a, b, _TN, preferred_element_type=F32)


def _dot_exact(a, b):
    return jnp.dot(a, b, preferred_element_type=F32, precision=lax.Precision.HIGHEST)


def _sigmoid(x):
    return 1.0 / (1.0 + jnp.exp(-x))


def _gate_sigmoid(x):
    return 0.5 * jnp.tanh(0.5 * x) + 0.5


def _layer_norm(x, g, b):
    mu = jnp.mean(x, axis=-1, keepdims=True)
    xc = x - mu
    var = jnp.mean(xc * xc, axis=-1, keepdims=True)
    return xc * lax.rsqrt(var + LN_EPS) * g + b


def _ln_in_body(x_ref, g_ref, b_ref, h_ref, hb_ref):
    h = _layer_norm(x_ref[...], g_ref[...], b_ref[...])
    h_ref[...] = h
    hb_ref[...] = h.astype(BF16)


def _ln_in(x2, g, b, tm=512):
    T, D = x2.shape
    return pl.pallas_call(
        _ln_in_body,
        grid=(T // tm,),
        in_specs=[pl.BlockSpec((tm, D), lambda i: (i, 0)),
                  pl.BlockSpec((1, D), lambda i: (0, 0)),
                  pl.BlockSpec((1, D), lambda i: (0, 0))],
        out_specs=[pl.BlockSpec((tm, D), lambda i: (i, 0)),
                   pl.BlockSpec((tm, D), lambda i: (i, 0))],
        out_shape=[jax.ShapeDtypeStruct((T, D), F32), jax.ShapeDtypeStruct((T, D), BF16)],
        compiler_params=_cparams(("arbitrary",)),
        name="ln_in",
    )(x2, g.reshape(1, D), b.reshape(1, D))


def _proj_body(x_ref, w_ref, b_ref, o_ref):
    o_ref[...] = (_dot(x_ref[...], w_ref[...]) + b_ref[...]).astype(o_ref.dtype)


def _proj(xb, w, b, out_dtype, tm, tn):
    T, K = xb.shape
    N = w.shape[1]
    return pl.pallas_call(
        _proj_body,
        grid=(N // tn, T // tm),
        in_specs=[pl.BlockSpec((tm, K), lambda j, i: (i, 0)),
                  pl.BlockSpec((K, tn), lambda j, i: (0, j)),
                  pl.BlockSpec((1, tn), lambda j, i: (0, j))],
        out_specs=pl.BlockSpec((tm, tn), lambda j, i: (i, j)),
        out_shape=jax.ShapeDtypeStruct((T, N), out_dtype),
        compiler_params=_cparams(("arbitrary", "arbitrary")),
        name="in_proj",
    )(xb, w, b.reshape(1, N))


def _conv_body(prev_ref, cur_ref, nxt_ref, w_ref, s_ref, o_ref, *, tr, ksz):
    i = pl.program_id(1)
    n = pl.num_programs(1)
    keep_prev = jnp.where(i > 0, 1.0, 0.0).astype(F32)
    keep_next = jnp.where(i < n - 1, 1.0, 0.0).astype(F32)
    xp = jnp.concatenate([prev_ref[0].astype(F32) * keep_prev,
                          cur_ref[0].astype(F32),
                          nxt_ref[0].astype(F32) * keep_next], axis=0)
    rows = tr + 16
    pad = ksz // 2
    acc = jnp.zeros((tr, xp.shape[1]), F32)
    for j in range(ksz):
        d = j - pad
        sh = xp if d == 0 else pltpu.roll(xp, (-d) % rows, 0)
        acc = acc + w_ref[j:j + 1, :] * sh[8:8 + tr, :]
    y = acc * _gate_sigmoid(acc)
    o_ref[0] = (y * s_ref[...]).astype(o_ref.dtype)


def _qk_conv(proj3, conv_w, scale, C, tr=512):
    B, S, _ = proj3.shape
    ksz = conv_w.shape[0]
    hb = tr // 8
    nhb = S // 8
    return pl.pallas_call(
        functools.partial(_conv_body, tr=tr, ksz=ksz),
        grid=(B, S // tr),
        in_specs=[pl.BlockSpec((1, 8, C), lambda b, i: (b, jnp.maximum(i * hb - 1, 0), 0)),
                  pl.BlockSpec((1, tr, C), lambda b, i: (b, i, 0)),
                  pl.BlockSpec((1, 8, C), lambda b, i: (b, jnp.minimum((i + 1) * hb, nhb - 1), 0)),
                  pl.BlockSpec((ksz, C), lambda b, i: (0, 0)),
                  pl.BlockSpec((1, C), lambda b, i: (0, 0))],
        out_specs=pl.BlockSpec((1, tr, C), lambda b, i: (b, i, 0)),
        out_shape=jax.ShapeDtypeStruct((B, S, C), BF16),
        compiler_params=_cparams(("arbitrary", "arbitrary")),
        name="qk_conv",
    )(proj3, proj3, proj3, conv_w, scale)


def _log_sigmoid(x):
    return jnp.minimum(x, 0.0) - jnp.log1p(jnp.exp(-jnp.abs(x)))


def _mlstm_dir(qk_ref, v_ref, g_ref, gt_ref, o_ref, c_ref, n_ref, m_ref, *, d, L, nh, dk, dv):
    row = lax.broadcasted_iota(I32, (L, L), 0)
    col = lax.broadcasted_iota(I32, (L, L), 1)
    if d == 0:
        causal = col <= row
        causal_t = col >= row
        last = L - 1
    else:
        causal = col >= row
        causal_t = col <= row
        last = 0
    gcols = g_ref[0]
    grows = gt_ref[0]
    cs_col = _dot_exact(jnp.where(causal, 1.0, 0.0).astype(F32), _log_sigmoid(gcols))
    cs_row = _dot_exact(_log_sigmoid(grows), jnp.where(causal_t, 1.0, 0.0).astype(F32))
    qk = qk_ref[0]
    v_all = v_ref[0]
    for h in range(nh):
        ci = d * nh + h
        cf = 2 * nh + d * nh + h
        sidx = d * nh + h
        gc = cs_col[:, cf:cf + 1]
        gr = cs_row[cf:cf + 1, :]
        lic = gcols[:, ci:ci + 1]
        lir = grows[ci:ci + 1, :]
        mprev = m_ref[sidx][0:1, 0:1]
        dm = jnp.where(causal, gc - gr + lir, -jnp.inf)
        inter = gc + mprev
        mt = jnp.maximum(jnp.max(dm, axis=1, keepdims=True), inter)
        q = qk[:, h * dk:(h + 1) * dk]
        k = qk[:, (nh + h) * dk:(nh + h + 1) * dk]
        v = v_all[:, h * dv:(h + 1) * dv]
        w = jnp.exp(dm - mt) * _dot_nt(q, k)
        sc = jnp.exp(inter - mt)
        cst = c_ref[sidx]
        nst = n_ref[sidx][0:1, :]
        num = _dot(w.astype(BF16), v) + sc * _dot(q, cst.astype(BF16))
        den = jnp.sum(w, axis=1, keepdims=True) + sc * jnp.sum(q.astype(F32) * nst, axis=1, keepdims=True)
        o_ref[0, :, h * dv:(h + 1) * dv] = num / jnp.maximum(jnp.abs(den), jnp.exp(-mt))
        mnew = mt[last:last + 1, :]
        gl = gc[last:last + 1, :]
        wk = jnp.exp(gl - gc + lic - mnew)
        decay = jnp.exp(gl + mprev - mnew)
        kw = k.astype(F32) * wk
        c_ref[sidx] = decay * cst + _dot_tn(kw.astype(BF16), v)
        n_ref[sidx] = jnp.broadcast_to(decay * nst + jnp.sum(kw, axis=0, keepdims=True), (8, dk))
        m_ref[sidx] = jnp.broadcast_to(mnew, (8, 128))


def _mlstm_body(qk_f, v_f, g_f, gt_f, qk_b, v_b, g_b, gt_b, of_ref, ob_ref, c_ref, n_ref, m_ref,
                *, L, nh, dk, dv):
    @pl.when(pl.program_id(1) == 0)
    def _():
        c_ref[...] = jnp.zeros_like(c_ref)
        n_ref[...] = jnp.zeros_like(n_ref)
        m_ref[...] = jnp.zeros_like(m_ref)

    kw = dict(L=L, nh=nh, dk=dk, dv=dv)
    _mlstm_dir(qk_f, v_f, g_f, gt_f, of_ref, c_ref, n_ref, m_ref, d=0, **kw)
    _mlstm_dir(qk_b, v_b, g_b, gt_b, ob_ref, c_ref, n_ref, m_ref, d=1, **kw)


def _mlstm(qk, proj3, v_blk, gates, gates_t, nh, dk, dv):
    B, S, _ = qk.shape
    L = MLSTM_CHUNK
    nc = S // L
    G = gates.shape[-1]
    fw = lambda b, j: (b, j, 0)
    bw = lambda b, j: (b, nc - 1 - j, 0)
    specs = []
    for im in (fw, bw):
        specs += [pl.BlockSpec((1, L, 2 * nh * dk), im),
                  pl.BlockSpec((1, L, nh * dv), (lambda b, j, im=im: (im(b, j)[0], im(b, j)[1], v_blk))),
                  pl.BlockSpec((1, L, G), im),
                  pl.BlockSpec((1, G, L), (lambda b, j, im=im: (im(b, j)[0], 0, im(b, j)[1])))]
    return pl.pallas_call(
        functools.partial(_mlstm_body, L=L, nh=nh, dk=dk, dv=dv),
        grid=(B, nc),
        in_specs=specs,
        out_specs=[pl.BlockSpec((1, L, nh * dv), fw), pl.BlockSpec((1, L, nh * dv), bw)],
        out_shape=[jax.ShapeDtypeStruct((B, S, nh * dv), F32)] * 2,
        scratch_shapes=[pltpu.VMEM((2 * nh, dk, dv), F32),
                        pltpu.VMEM((2 * nh, 8, dk), F32),
                        pltpu.VMEM((2 * nh, 8, 128), F32)],
        compiler_params=_cparams(("arbitrary", "arbitrary")),
        name="mlstm_scan",
    )(qk, proj3, gates, gates_t, qk, proj3, gates, gates_t)


def _hgrn_levels(L):
    c = L // 2
    out = []
    while c >= 1:
        out.append(c)
        c //= 2
    return out


def _hgrn_sum_matrix(L, d):
    t = np.arange(L)[:, None]
    r = np.arange(L)[None, :]
    if d == 0:
        blocks = [(r <= t), (r > t)]
    else:
        blocks = [(r >= t), (r < t)]
    for c in _hgrn_levels(L):
        base = (t // (2 * c)) * (2 * c)
        mid = base + c
        second = (t - base) >= c
        if d == 0:
            m = np.where(second, (r >= mid) & (r <= t), (r > t) & (r < mid))
        else:
            m = np.where(second, (r >= mid) & (r < t), (r >= t) & (r < mid))
        blocks.append(m)
    p = np.concatenate(blocks, axis=0).astype(np.float32)
    return np.concatenate([p, p, p], axis=1)


def _hgrn_prep(f_ref, lb_ref, p_ref, kb_ref, sums_ref, *, d, layer):
    slots = [lb_ref[k][d:d + 1, :] for k in range(lb_ref.shape[0])]
    top = functools.reduce(jnp.maximum, slots)
    es = [jnp.exp(s - top) for s in slots]
    lb = sum(es[:layer + 1]) / sum(es)
    f = lb + (1.0 - lb) * _sigmoid(f_ref[0])
    lgf = jnp.log2(f)
    kb_ref[d] = 1.0 - f
    hi = lgf.astype(BF16)
    r1 = lgf - hi.astype(F32)
    mid = r1.astype(BF16)
    lo = (r1 - mid.astype(F32)).astype(BF16)
    sums_ref[d] = _dot(p_ref[...], jnp.concatenate([hi, mid, lo], axis=0))


def _hgrn_head(q_ref, i_ref, o_ref, st_ref, kb_ref, sums_ref, h, *, d, L, nh, dk, dv):
    levels = _hgrn_levels(L)
    rowi = lax.broadcasted_iota(I32, (L, 1), 0)
    xor = lax.broadcasted_iota(I32, (L, L), 0) ^ lax.broadcasted_iota(I32, (L, L), 1)
    last = L - 1 if d == 0 else 0
    ks = slice(h * dk, (h + 1) * dk)
    vs = slice(h * dv, (h + 1) * dv)
    qh = q_ref[0, :, ks].astype(F32)
    kbh = kb_ref[d, :, ks]
    ih = i_ref[0, :, vs]
    eb = jnp.exp2(sums_ref[d, 0:L, ks])
    ea = jnp.exp2(sums_ref[d, L:2 * L, ks])
    a = jnp.zeros((L, L), F32)
    for li, c in enumerate(levels):
        k = c.bit_length() - 1
        bit = (rowi >> k) & 1
        if d == 1:
            bit = 1 - bit
        el = jnp.exp2(sums_ref[d, (2 + li) * L:(3 + li) * L, ks])
        z = (jnp.where(bit == 1, qh, kbh) * el).astype(BF16)
        pair = ((xor >> k) + ((1 - bit) << 8)) == 1
        a = jnp.where(pair, _dot_nt(z, z), a)
    st = st_ref[d * nh + h]
    o = (_dot(a.astype(BF16), ih) + _dot_nt((qh * eb).astype(BF16), st.astype(BF16))
         + jnp.sum(qh * kbh, axis=1, keepdims=True) * ih.astype(F32))
    o_ref[0, :, vs] = o
    st_ref[d * nh + h] = st * eb[last:last + 1, :] + _dot_tn(ih, (kbh * ea).astype(BF16))


def _hgrn_body(q_f, i_f, f_f, q_b, i_b, f_b, lb_ref, pf_ref, pb_ref, of_ref, ob_ref, st_ref,
               kb_ref, sums_ref, *, L, nh, dk, dv, layer):
    @pl.when(pl.program_id(1) == 0)
    def _():
        st_ref[...] = jnp.zeros_like(st_ref)

    _hgrn_prep(f_f, lb_ref, pf_ref, kb_ref, sums_ref, d=0, layer=layer)
    _hgrn_prep(f_b, lb_ref, pb_ref, kb_ref, sums_ref, d=1, layer=layer)
    kw = dict(L=L, nh=nh, dk=dk, dv=dv)

    for h in range(nh):
        _hgrn_head(q_f, i_f, of_ref, st_ref, kb_ref, sums_ref, h, d=0, **kw)
        _hgrn_head(q_b, i_b, ob_ref, st_ref, kb_ref, sums_ref, h, d=1, **kw)


def _hgrn(proj3, q_blk, i_blk, fpre, lb, layer, nh, dk, dv):
    B, S, _ = proj3.shape
    L = HGRN_CHUNK
    nc = S // L
    W = nh * dk
    pf = jnp.asarray(_hgrn_sum_matrix(L, 0), BF16)
    pb = jnp.asarray(_hgrn_sum_matrix(L, 1), BF16)
    fw = lambda b, j: (b, j)
    bw = lambda b, j: (b, nc - 1 - j)
    specs = []
    for d, im in enumerate((fw, bw)):
        specs += [pl.BlockSpec((1, L, W), (lambda b, j, im=im: im(b, j) + (q_blk,))),
                  pl.BlockSpec((1, L, nh * dv), (lambda b, j, im=im: im(b, j) + (i_blk,))),
                  pl.BlockSpec((1, L, W), (lambda b, j, im=im, d=d: im(b, j) + (d,)))]
    specs += [pl.BlockSpec(lb.shape, lambda b, j: (0, 0, 0)),
              pl.BlockSpec(pf.shape, lambda b, j: (0, 0)),
              pl.BlockSpec(pb.shape, lambda b, j: (0, 0))]
    return pl.pallas_call(
        functools.partial(_hgrn_body, L=L, nh=nh, dk=dk, dv=dv, layer=layer),
        grid=(B, nc),
        in_specs=specs,
        out_specs=[pl.BlockSpec((1, L, nh * dv), lambda b, j: (b, j, 0)),
                   pl.BlockSpec((1, L, nh * dv), lambda b, j: (b, nc - 1 - j, 0))],
        out_shape=[jax.ShapeDtypeStruct((B, S, nh * dv), F32)] * 2,
        scratch_shapes=[pltpu.VMEM((2 * nh, dv, dk), F32),
                        pltpu.VMEM((2, L, W), F32),
                        pltpu.VMEM((2, pf.shape[0], W), F32)],
        compiler_params=_cparams(("arbitrary", "arbitrary")),
        name="hgrn2_scan",
    )(proj3, proj3, fpre, proj3, proj3, fpre, lb, pf, pb)


def _head_rms(x, nh, dh):
    outs = []
    for h in range(nh):
        xh = x[:, h * dh:(h + 1) * dh]
        outs.append(xh * lax.rsqrt(jnp.mean(xh * xh, axis=-1, keepdims=True) + LN_EPS))
    return jnp.concatenate(outs, axis=1)


def _post_body(hmf, hmb, hof, hob, om, gh, gm, ghh, h_ref, mg, hg, wbm, wbh, wout, l1g, l1b, wr,
               h1_ref, aff_ref, *, nh_m, dv_m, nh_h, dv_h, alpha):
    hm = _head_rms(hmf[...] + hmb[...], nh_m, dv_m) * mg[...] * _gate_sigmoid(om[...].astype(F32))
    hgate = gh[...].astype(F32)
    ho = _head_rms(hof[...] + hob[...], nh_h, dv_h) * hg[...] * (hgate * _gate_sigmoid(hgate))
    y_m = _dot(hm.astype(BF16), wbm[...])
    y_h = _dot(ho.astype(BF16), wbh[...])
    merged = _gate_sigmoid(gm[...].astype(F32)) * y_m + _gate_sigmoid(ghh[...].astype(F32)) * y_h
    mix = _dot(merged.astype(BF16), wout[...])
    h1 = _layer_norm(alpha * h_ref[...] + mix, l1g[...], l1b[...])
    h1_ref[...] = h1
    logits = _dot_nt(wr[...], h1.astype(BF16))
    z = jnp.exp(logits - jnp.max(logits, axis=0, keepdims=True))
    aff_ref[0] = z / jnp.sum(z, axis=0, keepdims=True)


def _post_mixer(hm_f, hm_b, ho_f, ho_b, proj, cols, h, mg, hg, wbm, wbh, wout, l1g, l1b, wr_t,
                B, S, dims, alpha, tm=256):
    T, D = h.shape
    nh_m, dv_m, nh_h, dv_h = dims
    vm, vh = nh_m * dv_m, nh_h * dv_h
    E = wr_t.shape[0]
    nbs = S // tm
    row = lambda i: (i, 0)
    const = lambda i: (0, 0)
    in_specs = [pl.BlockSpec((tm, vm), row), pl.BlockSpec((tm, vm), row),
                pl.BlockSpec((tm, vh), row), pl.BlockSpec((tm, vh), row),
                pl.BlockSpec((tm, vm), lambda i: (i, cols["o_m"] // vm)),
                pl.BlockSpec((tm, vh), lambda i: (i, cols["g_h"] // vh)),
                pl.BlockSpec((tm, D), lambda i: (i, cols["gate_m"] // D)),
                pl.BlockSpec((tm, D), lambda i: (i, cols["gate_h"] // D)),
                pl.BlockSpec((tm, D), row),
                pl.BlockSpec((1, vm), const), pl.BlockSpec((1, vh), const),
                pl.BlockSpec((vm, D), const), pl.BlockSpec((vh, D), const), pl.BlockSpec((D, D), const),
                pl.BlockSpec((1, D), const), pl.BlockSpec((1, D), const),
                pl.BlockSpec((E, D), const)]
    return pl.pallas_call(
        functools.partial(_post_body, nh_m=nh_m, dv_m=dv_m, nh_h=nh_h, dv_h=dv_h, alpha=alpha),
        grid=(T // tm,),
        in_specs=in_specs,
        out_specs=[pl.BlockSpec((tm, D), row),
                   pl.BlockSpec((1, E, tm), lambda i: (i // nbs, 0, i % nbs))],
        out_shape=[jax.ShapeDtypeStruct((T, D), F32), jax.ShapeDtypeStruct((B, E, S), F32)],
        compiler_params=_cparams(("arbitrary",)),
        name="mixer_out_ln1_router",
    )(hm_f, hm_b, ho_f, ho_b, proj, proj, proj, proj, h, mg, hg, wbm, wbh, wout, l1g, l1b, wr_t)


def _prefix_counts(mask2, upper, ones, bdl):
    mb = mask2.astype(BF16)
    within = _dot(mb, upper)
    rowtot = _dot(mb, ones)
    before = _dot(bdl, rowtot.astype(BF16))
    return within + before, within


def _topk_body(a_ref, upper_ref, ones_ref, bdl_ref, idx_ref, pos_ref, off_ref, *, E, R, cap):
    a3 = a_ref[0]

    def count(m):
        return jnp.sum(jnp.sum(jnp.where(m, 1.0, 0.0), axis=2, keepdims=True), axis=1, keepdims=True)

    def as_float(bits):
        return pltpu.bitcast(jnp.broadcast_to(bits, a3.shape), F32)

    def step(it, cur):
        cand = cur | (jnp.int32(1) << (30 - it))
        return jnp.where(count(a3 >= as_float(cand)) >= cap, cand, cur)

    thr = lax.fori_loop(0, 31, step, jnp.zeros((E, 1, 1), I32))
    gt = a3 >= as_float(thr + 1)
    eq = jnp.logical_and(a3 >= as_float(thr), jnp.logical_not(gt))
    need = cap - count(gt)
    upper, ones, bdl = upper_ref[...], ones_ref[...], bdl_ref[...]
    eq_incl, _ = _prefix_counts(jnp.where(eq, 1.0, 0.0).reshape(E * R, 128), upper, ones, bdl)
    sel = jnp.logical_or(gt, jnp.logical_and(eq, eq_incl.reshape(E, R, 128) <= need))
    self32 = jnp.where(sel, 1.0, 0.0).reshape(E * R, 128)
    incl, within = _prefix_counts(self32, upper, ones, bdl)
    pos_ref[0] = jnp.where(sel, incl.reshape(E, R, 128) - 1.0, -1.0).astype(I32)

    slot = lax.broadcasted_iota(I32, (1, cap), 1).astype(F32)
    rsub = lax.broadcasted_iota(I32, (R, cap), 0).astype(F32)
    for e in range(E):
        inc_e = incl[e * R:(e + 1) * R]
        row_incl = inc_e[:, 127:128]
        row_excl = row_incl - within[e * R:(e + 1) * R][:, 127:128]
        off_ref[0, e] = jnp.broadcast_to(row_excl, (R, 128)).astype(I32)
        ridx = jnp.sum(jnp.where(row_incl <= slot, 1.0, 0.0), axis=0, keepdims=True)
        onehot = rsub == ridx
        start = jnp.sum(jnp.where(onehot, row_excl, 0.0), axis=0, keepdims=True)
        local = slot - start
        pg = _dot_tn(within[e * R:(e + 1) * R].astype(BF16), jnp.where(onehot, 1.0, 0.0).astype(BF16))
        lane = jnp.sum(jnp.where(pg <= local, 1.0, 0.0), axis=0, keepdims=True)
        idx_ref[0, e:e + 1, :] = (ridx * 128.0 + lane).astype(I32)


def _topk(aff_t, cap):
    B, E, S = aff_t.shape
    R = S // 128
    k = np.arange(128)
    upper = jnp.asarray((k[:, None] <= k[None, :]).astype(np.float32), BF16)
    ones = jnp.ones((128, 128), BF16)
    r = np.arange(E * R)
    bdl = jnp.asarray(((r[:, None] // R == r[None, :] // R) & (r[None, :] < r[:, None])).astype(np.float32), BF16)
    const2 = lambda b: (0, 0)
    return pl.pallas_call(
        functools.partial(_topk_body, E=E, R=R, cap=cap),
        grid=(B,),
        in_specs=[pl.BlockSpec((1, E, R, 128), lambda b: (b, 0, 0, 0)),
                  pl.BlockSpec((128, 128), const2), pl.BlockSpec((128, 128), const2),
                  pl.BlockSpec((E * R, E * R), const2)],
        out_specs=[pl.BlockSpec((1, E, cap), lambda b: (b, 0, 0)),
                   pl.BlockSpec((1, E, R, 128), lambda b: (b, 0, 0, 0)),
                   pl.BlockSpec((1, E, R, 128), lambda b: (b, 0, 0, 0))],
        out_shape=[jax.ShapeDtypeStruct((B, E, cap), I32),
                   jax.ShapeDtypeStruct((B, E, R, 128), I32),
                   jax.ShapeDtypeStruct((B, E, R, 128), I32)],
        compiler_params=_cparams(("arbitrary",)),
        name="expert_topk",
    )(aff_t.reshape(B, E, R, 128), upper, ones, bdl)


def _ffn_body(idx_ref, h_hbm, wg_ref, wu_ref, wd_ref, ye_ref, xg, xb, acc, sem, *, E, B, C, rps):
    e = pl.program_id(0)
    b = pl.program_id(1)
    f = pl.program_id(2)
    nf = pl.num_programs(2)
    nrows = xg.shape[1]
    pair = e * B + b
    slot = pair % 2
    nxt = (pair + 1) % (E * B)
    e_n = nxt // B
    b_n = nxt % B

    def row_copy(bb, src_row, sl, dst_row):
        return pltpu.make_async_copy(h_hbm.at[bb, pl.ds(src_row, 1), :],
                                     xg.at[sl, pl.ds(dst_row, 1), :], sem.at[sl])

    def src_row(bb, ee, s):
        return idx_ref[(bb * E + ee) * C + jnp.minimum(s, C - 1)]

    def wait_all(sl):
        def wait(s, carry):
            row_copy(0, 0, sl, s).wait()
            return carry
        lax.fori_loop(0, nrows, wait, 0, unroll=rps)

    @pl.when(f == 0)
    def _():
        @pl.when(pair == 0)
        def _():
            def issue(s, carry):
                row_copy(b, src_row(b, e, s), slot, s).start()
                return carry
            lax.fori_loop(0, nrows, issue, 0)

        wait_all(slot)
        xb[...] = xg[slot, 0:C, :].astype(BF16)
        acc[...] = jnp.zeros_like(acc)

    for r in range(rps):
        s = f * rps + r
        row_copy(b_n, src_row(b_n, e_n, s), 1 - slot, s).start()

    x = xb[...]
    g = _dot(x, wg_ref[0].astype(BF16))
    u = _dot(x, wu_ref[0].astype(BF16))
    hid = (g * _gate_sigmoid(g) * u).astype(BF16)
    acc[...] += _dot(hid, wd_ref[0].astype(BF16))

    @pl.when(f == nf - 1)
    def _():
        ye_ref[0, 0] = acc[...].astype(ye_ref.dtype)

        @pl.when(pair == E * B - 1)
        def _():
            wait_all(1 - slot)


def _expert_ffn(idx, h1, w_gate, w_up, w_down, tf=256):
    B, S, D = h1.shape
    E, _, F = w_gate.shape
    C = idx.shape[-1]
    nf = F // tf
    rps = pl.cdiv(C, nf)
    grid_spec = pltpu.PrefetchScalarGridSpec(
        num_scalar_prefetch=1,
        grid=(E, B, nf),
        in_specs=[pl.BlockSpec(memory_space=pl.ANY),
                  pl.BlockSpec((1, D, tf), lambda e, b, f, idx: (e, 0, f)),
                  pl.BlockSpec((1, D, tf), lambda e, b, f, idx: (e, 0, f)),
                  pl.BlockSpec((1, tf, D), lambda e, b, f, idx: (e, f, 0))],
        out_specs=pl.BlockSpec((1, 1, C, D), lambda e, b, f, idx: (b, e, 0, 0)),
        scratch_shapes=[pltpu.VMEM((2, nf * rps, D), F32), pltpu.VMEM((C, D), BF16),
                        pltpu.VMEM((C, D), F32), pltpu.SemaphoreType.DMA((2,))],
    )
    return pl.pallas_call(
        functools.partial(_ffn_body, E=E, B=B, C=C, rps=rps),
        grid_spec=grid_spec,
        out_shape=jax.ShapeDtypeStruct((B, E, C, D), BF16),
        compiler_params=_cparams(("arbitrary", "arbitrary", "arbitrary")),
        name="expert_ffn",
    )(idx.reshape(-1), h1, w_gate, w_up, w_down)


COMBINE_ROWS = 16
COMBINE_KT = 256


def _combine_body(st_ref, h_ref, pos_ref, aff_ref, ye_hbm, g_ref, b_ref, o_ref, buf, acc, sem,
                  *, E, B, TB, NTB, alpha):
    b = pl.program_id(0)
    tb = pl.program_id(1)
    rows, kt = COMBINE_ROWS, COMBINE_KT
    step = b * NTB + tb
    slot = step % 2

    def chunk_copy(bb, e, src, sl, dst):
        return pltpu.make_async_copy(ye_hbm.at[bb, e, pl.ds(src, rows), :],
                                     buf.at[sl, pl.ds(dst, rows), :], sem.at[sl])

    def plan(bb, tt):
        koff = jnp.int32(0)
        out = []
        for e in range(E):
            base = (bb * E + e) * (NTB + 1) + tt
            ws = (st_ref[base] // rows) * rows
            n = (st_ref[base + 1] - ws + rows - 1) // rows
            out.append((ws, n, koff))
            koff = koff + n * rows
        return out, koff

    def issue_all(bb, tt, sl):
        for e, (ws, n, koff) in enumerate(plan(bb, tt)[0]):
            def issue(c, carry, e=e, ws=ws, koff=koff):
                chunk_copy(bb, e, pl.multiple_of(ws + c * rows, rows), sl,
                           pl.multiple_of(koff + c * rows, rows)).start()
                return carry
            lax.fori_loop(0, n, issue, 0)

    @pl.when(step == 0)
    def _():
        buf[...] = jnp.zeros_like(buf)
        issue_all(b, tb, slot)

    cur, ktot = plan(b, tb)

    def wait(c, carry):
        chunk_copy(0, 0, 0, slot, 0).wait()
        return carry

    lax.fori_loop(0, ktot // rows, wait, 0)

    @pl.when(step + 1 < B * NTB)
    def _():
        issue_all((step + 1) // NTB, (step + 1) % NTB, 1 - slot)

    pos = pos_ref[0]
    aff = aff_ref[0]
    keys = []
    for e, (ws, n, koff) in enumerate(cur):
        pe = pos[e:e + 1, :]
        keys.append(jnp.where(pe >= 0, pe + (koff - ws), -1))

    def onehot_t(kbase):
        ksub = lax.broadcasted_iota(I32, (kt, TB), 0) + kbase
        o = jnp.zeros((kt, TB), F32)
        for e in range(E):
            o = jnp.where(ksub == keys[e], aff[e:e + 1, :], o)
        return o.astype(BF16)

    acc[...] = _dot_tn(onehot_t(0), buf[slot, 0:kt, :])

    def kstep(kc, carry):
        kbase = pl.multiple_of(kc * kt, kt)
        acc[...] += _dot_tn(onehot_t(kbase), buf[slot, pl.ds(kbase, kt), :])
        return carry

    lax.fori_loop(1, (ktot + kt - 1) // kt, kstep, 0)
    o_ref[0] = _layer_norm(alpha * h_ref[0] + acc[...], g_ref[...], b_ref[...])


def _combine(starts, h1, pos, aff, ye, g, bb, alpha, TB):
    B, S, D = h1.shape
    E = ye.shape[1]
    NTB = S // TB
    kmax = pl.cdiv(E * (TB + 2 * COMBINE_ROWS), COMBINE_KT) * COMBINE_KT
    grid_spec = pltpu.PrefetchScalarGridSpec(
        num_scalar_prefetch=1,
        grid=(B, NTB),
        in_specs=[pl.BlockSpec((1, TB, D), lambda b, t, st: (b, t, 0)),
                  pl.BlockSpec((1, E, TB), lambda b, t, st: (b, 0, t)),
                  pl.BlockSpec((1, E, TB), lambda b, t, st: (b, 0, t)),
                  pl.BlockSpec(memory_space=pl.ANY),
                  pl.BlockSpec((1, D), lambda b, t, st: (0, 0)),
                  pl.BlockSpec((1, D), lambda b, t, st: (0, 0))],
        out_specs=pl.BlockSpec((1, TB, D), lambda b, t, st: (b, t, 0)),
        scratch_shapes=[pltpu.VMEM((2, kmax, D), BF16), pltpu.VMEM((TB, D), F32),
                        pltpu.SemaphoreType.DMA((2,))],
    )
    return pl.pallas_call(
        functools.partial(_combine_body, E=E, B=B, TB=TB, NTB=NTB, alpha=alpha),
        grid_spec=grid_spec,
        out_shape=jax.ShapeDtypeStruct((B, S, D), F32),
        compiler_params=_cparams(("arbitrary", "arbitrary")),
        name="combine_ln2",
    )(starts.reshape(-1), h1, pos, aff, ye, g.reshape(1, D), bb.reshape(1, D))


def _layer(h, hb, B, S, lb, layer, w_in, b_in, conv_w, mg, hg, w_bm, w_bh, w_out, l1g, l1b, w_router,
           w_gate, w_up, w_down, l2g, l2b, alpha):
    T, D = h.shape
    nh_m, dv_m = mg.shape
    nh_h, dv_h = hg.shape
    ksz, qk2 = conv_w.shape
    dk_m = qk2 // (2 * nh_m)
    v_m = nh_m * dv_m
    v_h = nh_h * dv_h
    q_h = lb.shape[-1]
    dk_h = q_h // nh_h
    E = w_router.shape[1]
    ng = 4 * nh_m
    sizes = (qk2, v_m, v_m, ng // 2, ng // 2, q_h, 2 * q_h, v_h, v_h, D, D)
    names = ("qk_m", "v_m", "o_m", "i_m", "f_m", "q_h", "f_h", "i_h", "g_h", "gate_m", "gate_h")
    start = dict(zip(names, np.concatenate([[0], np.cumsum(sizes)[:-1]]).tolist()))
    size = dict(zip(names, sizes))
    sl = lambda a, n: a[..., start[n]:start[n] + size[n]]
    a_names = ("qk_m", "v_m", "o_m", "q_h", "i_h", "g_h", "gate_m", "gate_h")
    w_a = jnp.concatenate([sl(w_in, n) for n in a_names], axis=1).astype(BF16)
    b_a = jnp.concatenate([sl(b_in, n) for n in a_names])
    cols = dict(zip(a_names, np.concatenate([[0], np.cumsum([size[n] for n in a_names])[:-1]]).tolist()))
    w_f = sl(w_in, "f_h").astype(BF16)
    b_f = sl(b_in, "f_h")
    gpad = 128 - ng
    w_g = jnp.pad(jnp.concatenate([sl(w_in, "i_m"), sl(w_in, "f_m")], axis=1), ((0, 0), (0, gpad))).astype(BF16)
    b_g = jnp.pad(jnp.concatenate([sl(b_in, "i_m"), sl(b_in, "f_m")]), (0, gpad))

    proj = _proj(hb, w_a, b_a, BF16, tm=1024, tn=1024)
    fpre = _proj(hb, w_f, b_f, F32, tm=1024, tn=1024)
    gates = _proj(hb, w_g, b_g, F32, tm=1024, tn=128)[:, :ng].reshape(B, S, ng)
    proj3 = proj.reshape(B, S, -1)

    kscale = jnp.concatenate([jnp.ones((qk2 // 2,), F32), jnp.full((qk2 // 2,), dk_m ** -0.5, F32)])
    qk = _qk_conv(proj3, conv_w, kscale.reshape(1, qk2), qk2)
    hm_f, hm_b = _mlstm(qk, proj3, cols["v_m"] // v_m, gates, jnp.swapaxes(gates, 1, 2), nh_m, dk_m, dv_m)

    ho_f, ho_b = _hgrn(proj3, cols["q_h"] // q_h, cols["i_h"] // v_h, fpre.reshape(B, S, 2 * q_h), lb,
                       layer, nh_h, dk_h, dv_h)

    h1, aff_t = _post_mixer(hm_f.reshape(T, v_m), hm_b.reshape(T, v_m), ho_f.reshape(T, v_h),
                            ho_b.reshape(T, v_h), proj, cols, h, mg.reshape(1, v_m), hg.reshape(1, v_h),
                            w_bm.astype(BF16), w_bh.astype(BF16), w_out.astype(BF16),
                            l1g.reshape(1, D), l1b.reshape(1, D), w_router.T.astype(BF16),
                            B, S, (nh_m, dv_m, nh_h, dv_h), alpha)

    cap = CAP_FACTOR * S // E
    idx, pos, off = _topk(aff_t, cap)
    h1_3 = h1.reshape(B, S, D)
    ye = _expert_ffn(idx, h1_3, w_gate, w_up, w_down)
    TB = 256
    starts = jnp.concatenate([off[:, :, ::TB // 128, 0], jnp.full((B, E, 1), cap, I32)], axis=2)
    out = _combine(starts, h1_3, pos.reshape(B, E, S), aff_t, ye, l2g, l2b, alpha, TB=TB)
    return out.reshape(T, D)


def kernel(x, ln_in_g, ln_in_b, hgrn_lb_logits, w_in, b_in, conv_w, mlstm_norm_g, hgrn_norm_g,
           w_branch_m, w_branch_h, w_out, ln1_g, ln1_b, w_router, w_gate_e, w_up_e, w_down_e,
           ln2_g, ln2_b):
    B, S, D = x.shape
    depth = w_in.shape[0]
    alpha = (2.0 * depth) ** 0.25
    h, hb = _ln_in(x.reshape(B * S, D), ln_in_g, ln_in_b)
    lb_logits = jnp.swapaxes(hgrn_lb_logits.astype(F32), 0, 1)
    for l in range(depth):
        h = _layer(h, hb, B, S, lb_logits, l, w_in[l], b_in[l], conv_w[l], mlstm_norm_g[l],
                   hgrn_norm_g[l], w_branch_m[l], w_branch_h[l], w_out[l], ln1_g[l], ln1_b[l],
                   w_router[l], w_gate_e[l], w_up_e[l], w_down_e[l], ln2_g[l], ln2_b[l], alpha)
        if l + 1 < depth:
            hb = h.astype(BF16)
    return h.reshape(B, S, D)
```

```python
import functools

import numpy as np
import jax
import jax.numpy as jnp
from jax import lax
from jax.experimental import pallas as pl
from jax.experimental.pallas import tpu as pltpu

F32 = jnp.float32
BF16 = jnp.bfloat16
I32 = jnp.int32

LN_EPS = 1e-5
CAP_FACTOR = 2
MLSTM_CHUNK = 256
HGRN_CHUNK = 64
VMEM_LIMIT = 56 * 1024 * 1024

_NT = (((1,), (1,)), ((), ()))
_TN = (((0,), (0,)), ((), ()))


def _cparams(sem):
    return pltpu.CompilerParams(dimension_semantics=sem, vmem_limit_bytes=VMEM_LIMIT)


def _dot(a, b):
    return jnp.dot(a, b, preferred_element_type=F32)


def _dot_nt(a, b):
    return lax.dot_general(a, b, _NT, preferred_element_type=F32)


def _dot_tn(a, b):
    return lax.dot_general(a, b, _TN, preferred_element_type=F32)


def _dot_exact(a, b):
    return jnp.dot(a, b, preferred_element_type=F32, precision=lax.Precision.HIGHEST)


def _sigmoid(x):
    return 1.0 / (1.0 + jnp.exp(-x))


def _gate_sigmoid(x):
    return 0.5 * jnp.tanh(0.5 * x) + 0.5


def _layer_norm(x, g, b):
    mu = jnp.mean(x, axis=-1, keepdims=True)
    xc = x - mu
    var = jnp.mean(xc * xc, axis=-1, keepdims=True)
    return xc * lax.rsqrt(var + LN_EPS) * g + b


def _ln_in_body(x_ref, g_ref, b_ref, hb_ref):
    hb_ref[...] = _layer_norm(x_ref[...], g_ref[...], b_ref[...]).astype(BF16)


def _ln_in(x2, g, b, tm=512):
    T, D = x2.shape
    return pl.pallas_call(
        _ln_in_body,
        grid=(T // tm,),
        in_specs=[pl.BlockSpec((tm, D), lambda i: (i, 0)),
                  pl.BlockSpec((1, D), lambda i: (0, 0)),
                  pl.BlockSpec((1, D), lambda i: (0, 0))],
        out_specs=pl.BlockSpec((tm, D), lambda i: (i, 0)),
        out_shape=jax.ShapeDtypeStruct((T, D), BF16),
        compiler_params=_cparams(("arbitrary",)),
        name="ln_in",
    )(x2, g, b)


def _proj_body(x_ref, w_ref, b_ref, wg_ref, bg_ref, a_ref, f_ref, g_ref, *, na, nf):
    j = pl.program_id(1)

    @pl.when(j < na)
    def _():
        a_ref[...] = (_dot(x_ref[...], w_ref[...]) + b_ref[...]).astype(a_ref.dtype)

    @pl.when(jnp.logical_and(j >= na, j < na + nf))
    def _():
        f_ref[...] = _dot(x_ref[...], w_ref[...]) + b_ref[...]

    @pl.when(j == na + nf)
    def _():
        g_ref[...] = _dot(x_ref[...], wg_ref[...]) + bg_ref[...]


def _proj(xb, w, b, w_g, b_g, n_a, tm=1024, tn=1024):
    T, K = xb.shape
    N = w.shape[1]
    na, nf = n_a // tn, (N - n_a) // tn
    G = w_g.shape[1]
    last = na + nf - 1
    return pl.pallas_call(
        functools.partial(_proj_body, na=na, nf=nf),
        grid=(T // tm, na + nf + 1),
        in_specs=[pl.BlockSpec((tm, K), lambda i, j: (i, 0)),
                  pl.BlockSpec((K, tn), lambda i, j: (0, jnp.minimum(j, last))),
                  pl.BlockSpec((1, tn), lambda i, j: (0, jnp.minimum(j, last))),
                  pl.BlockSpec((K, G), lambda i, j: (0, 0)),
                  pl.BlockSpec((1, G), lambda i, j: (0, 0))],
        out_specs=[pl.BlockSpec((tm, tn), lambda i, j: (i, jnp.minimum(j, na - 1))),
                   pl.BlockSpec((tm, tn), lambda i, j: (i, jnp.clip(j - na, 0, nf - 1))),
                   pl.BlockSpec((tm, G), lambda i, j: (i, 0))],
        out_shape=[jax.ShapeDtypeStruct((T, n_a), BF16),
                   jax.ShapeDtypeStruct((T, N - n_a), F32),
                   jax.ShapeDtypeStruct((T, G), F32)],
        compiler_params=_cparams(("arbitrary", "arbitrary")),
        name="in_proj",
    )(xb, w, b.reshape(1, N), w_g, b_g.reshape(1, G))


def _conv_body(prev_ref, cur_ref, nxt_ref, w_ref, s_ref, o_ref, *, tr, ksz):
    i = pl.program_id(1)
    n = pl.num_programs(1)
    keep_prev = jnp.where(i > 0, 1.0, 0.0).astype(F32)
    keep_next = jnp.where(i < n - 1, 1.0, 0.0).astype(F32)
    xp = jnp.concatenate([prev_ref[0].astype(F32) * keep_prev,
                          cur_ref[0].astype(F32),
                          nxt_ref[0].astype(F32) * keep_next], axis=0)
    rows = tr + 16
    pad = ksz // 2
    acc = jnp.zeros((tr, xp.shape[1]), F32)
    for j in range(ksz):
        d = j - pad
        sh = xp if d == 0 else pltpu.roll(xp, (-d) % rows, 0)
        acc = acc + w_ref[j:j + 1, :] * sh[8:8 + tr, :]
    y = acc * _gate_sigmoid(acc)
    o_ref[0] = (y * s_ref[...]).astype(o_ref.dtype)


def _qk_conv(proj3, conv_w, scale, C, tr=512):
    B, S, _ = proj3.shape
    ksz = conv_w.shape[0]
    hb = tr // 8
    nhb = S // 8
    return pl.pallas_call(
        functools.partial(_conv_body, tr=tr, ksz=ksz),
        grid=(B, S // tr),
        in_specs=[pl.BlockSpec((1, 8, C), lambda b, i: (b, jnp.maximum(i * hb - 1, 0), 0)),
                  pl.BlockSpec((1, tr, C), lambda b, i: (b, i, 0)),
                  pl.BlockSpec((1, 8, C), lambda b, i: (b, jnp.minimum((i + 1) * hb, nhb - 1), 0)),
                  pl.BlockSpec((ksz, C), lambda b, i: (0, 0)),
                  pl.BlockSpec((1, C), lambda b, i: (0, 0))],
        out_specs=pl.BlockSpec((1, tr, C), lambda b, i: (b, i, 0)),
        out_shape=jax.ShapeDtypeStruct((B, S, C), BF16),
        compiler_params=_cparams(("arbitrary", "arbitrary")),
        name="qk_conv",
    )(proj3, proj3, proj3, conv_w, scale)


def _log_sigmoid(x):
    return jnp.minimum(x, 0.0) - jnp.log1p(jnp.exp(-jnp.abs(x)))


def _mlstm_stage1(g_ref, gt_ref, e_ref, rows_ref, slot, *, d, L, nh, dk):
    row = lax.broadcasted_iota(I32, (L, L), 0)
    col = lax.broadcasted_iota(I32, (L, L), 1)
    if d == 0:
        causal = col <= row
        causal_t = col >= row
        last = L - 1
    else:
        causal = col >= row
        causal_t = col <= row
        last = 0
    gcols = g_ref[0]
    grows = gt_ref[0]
    cs_col = _dot_exact(jnp.where(causal, 1.0, 0.0).astype(F32), _log_sigmoid(gcols))
    cs_row = _dot_exact(_log_sigmoid(grows), jnp.where(causal_t, 1.0, 0.0).astype(F32))
    for h in range(nh):
        ci = d * nh + h
        cf = 2 * nh + d * nh + h
        sidx = d * nh + h
        gc = cs_col[:, cf:cf + 1]
        dm = jnp.where(causal, gc - cs_row[cf:cf + 1, :] + grows[ci:ci + 1, :], -jnp.inf)
        rm = jnp.max(dm, axis=1, keepdims=True)
        e_ref[slot, sidx] = jnp.exp(dm - rm)
        rows_ref[slot, sidx, 0] = jnp.broadcast_to(rm, (L, dk))
        rows_ref[slot, sidx, 1] = jnp.broadcast_to(gc, (L, dk))
        rows_ref[slot, sidx, 2] = jnp.broadcast_to(gc[last:last + 1, :] - gc + gcols[:, ci:ci + 1], (L, dk))


def _mlstm_stage2(qk_ref, v_ref, o_ref, c_ref, n_ref, m_ref, e_ref, rows_ref, slot, *, d, L, nh, dk, dv):
    last = L - 1 if d == 0 else 0
    rep = dv // dk
    wide = lambda x: jnp.concatenate([x] * rep, axis=1)
    ones = jnp.ones((L, dk), BF16)
    qk = qk_ref[0]
    v_all = v_ref[0]
    for h in range(nh):
        sidx = d * nh + h
        rm = rows_ref[slot, sidx, 0]
        gc = rows_ref[slot, sidx, 1]
        mprev = m_ref[sidx][0:1, :]
        inter = gc + mprev
        mt = jnp.maximum(rm, inter)
        corr = jnp.exp(rm - mt)
        sc = jnp.exp(inter - mt)
        q = qk[:, h * dk:(h + 1) * dk]
        k = qk[:, (nh + h) * dk:(nh + h + 1) * dk]
        v = v_all[:, h * dv:(h + 1) * dv]
        p = (e_ref[slot, sidx] * _dot_nt(q, k)).astype(BF16)
        cst = c_ref[sidx]
        nst = n_ref[sidx]
        num = wide(corr) * _dot(p, v) + wide(sc) * _dot(q, cst.astype(BF16))
        qn = _dot_nt(q, jnp.concatenate([nst] * (dk // 8), axis=0).astype(BF16))
        den = corr * _dot(p, ones) + sc * qn
        o_ref[0, :, h * dv:(h + 1) * dv] = num / wide(jnp.maximum(jnp.abs(den), jnp.exp(-mt)))
        mnew = mt[last:last + 1, :]
        wk = jnp.exp(rows_ref[slot, sidx, 2] - mnew)
        decay = jnp.exp(gc[last:last + 1, :] + mprev - mnew)
        kw = k.astype(F32) * wk
        c_ref[sidx] = wide(decay) * cst + _dot_tn(kw.astype(BF16), v)
        n_ref[sidx] = jnp.broadcast_to(decay * nst[0:1, :] + jnp.sum(kw, axis=0, keepdims=True), (8, dk))
        m_ref[sidx] = jnp.broadcast_to(mnew, (8, dk))


def _mlstm_body(g_f, gt_f, g_b, gt_b, qk_f, v_f, qk_b, v_b, of_ref, ob_ref, c_ref, n_ref, m_ref,
                e_ref, rows_ref, *, L, nh, dk, dv):
    j = pl.program_id(1)

    @pl.when(j == 0)
    def _():
        e_ref[1] = jnp.zeros(e_ref.shape[1:], F32)
        rows_ref[1] = jnp.zeros(rows_ref.shape[1:], F32)

    @pl.when(j <= 1)
    def _():
        c_ref[...] = jnp.zeros_like(c_ref)
        n_ref[...] = jnp.zeros_like(n_ref)
        m_ref[...] = jnp.zeros_like(m_ref)

    def step(cur):
        for d, (g, gt) in enumerate(((g_f, gt_f), (g_b, gt_b))):
            _mlstm_stage1(g, gt, e_ref, rows_ref, cur, d=d, L=L, nh=nh, dk=dk)
        for d, (qk, v, o) in enumerate(((qk_f, v_f, of_ref), (qk_b, v_b, ob_ref))):
            _mlstm_stage2(qk, v, o, c_ref, n_ref, m_ref, e_ref, rows_ref, 1 - cur,
                          d=d, L=L, nh=nh, dk=dk, dv=dv)

    for parity in (0, 1):
        pl.when(j % 2 == parity)(functools.partial(step, parity))


def _mlstm(qk, proj3, v_blk, gates, gates_t, nh, dk, dv):
    B, S, _ = qk.shape
    L = MLSTM_CHUNK
    nc = S // L
    G = gates.shape[-1]
    assert dv % dk == 0 and dk % 8 == 0
    prep_f = lambda b, j: (b, jnp.minimum(j, nc - 1))
    prep_b = lambda b, j: (b, jnp.maximum(nc - 1 - j, 0))
    run_f = lambda b, j: (b, jnp.maximum(j - 1, 0))
    run_b = lambda b, j: (b, jnp.minimum(nc - j, nc - 1))
    specs = []
    for im in (prep_f, prep_b):
        specs += [pl.BlockSpec((1, L, G), (lambda b, j, im=im: im(b, j) + (0,))),
                  pl.BlockSpec((1, G, L), (lambda b, j, im=im: (im(b, j)[0], 0, im(b, j)[1])))]
    for im in (run_f, run_b):
        specs += [pl.BlockSpec((1, L, 2 * nh * dk), (lambda b, j, im=im: im(b, j) + (0,))),
                  pl.BlockSpec((1, L, nh * dv), (lambda b, j, im=im: im(b, j) + (v_blk,)))]
    return pl.pallas_call(
        functools.partial(_mlstm_body, L=L, nh=nh, dk=dk, dv=dv),
        grid=(B, nc + 1),
        in_specs=specs,
        out_specs=[pl.BlockSpec((1, L, nh * dv), lambda b, j: run_f(b, j) + (0,)),
                   pl.BlockSpec((1, L, nh * dv), lambda b, j: run_b(b, j) + (0,))],
        out_shape=[jax.ShapeDtypeStruct((B, S, nh * dv), F32)] * 2,
        scratch_shapes=[pltpu.VMEM((2 * nh, dk, dv), F32),
                        pltpu.VMEM((2 * nh, 8, dk), F32),
                        pltpu.VMEM((2 * nh, 8, dk), F32),
                        pltpu.VMEM((2, 2 * nh, L, L), F32),
                        pltpu.VMEM((2, 2 * nh, 3, L, dk), F32)],
        compiler_params=_cparams(("arbitrary", "arbitrary")),
        name="mlstm_scan",
    )(gates, gates_t, gates, gates_t, qk, proj3, qk, proj3)


def _hgrn_levels(L):
    c = L // 2
    out = []
    while c >= 1:
        out.append(c)
        c //= 2
    return out


def _hgrn_sum_matrix(L, d):
    t = np.arange(L)[:, None]
    r = np.arange(L)[None, :]
    if d == 0:
        blocks = [(r <= t), (r > t)]
    else:
        blocks = [(r >= t), (r < t)]
    for c in _hgrn_levels(L):
        base = (t // (2 * c)) * (2 * c)
        mid = base + c
        second = (t - base) >= c
        if d == 0:
            m = np.where(second, (r >= mid) & (r <= t), (r > t) & (r < mid))
        else:
            m = np.where(second, (r >= mid) & (r < t), (r >= t) & (r < mid))
        blocks.append(m)
    p = np.concatenate(blocks, axis=0).astype(np.float32)
    return np.concatenate([p, p, p], axis=1)


def _hgrn_prep(f_ref, lb_ref, p_ref, kb_ref, sums_ref, *, d, layer):
    slots = [lb_ref[k][d:d + 1, :] for k in range(lb_ref.shape[0])]
    top = functools.reduce(jnp.maximum, slots)
    es = [jnp.exp(s - top) for s in slots]
    lb = sum(es[:layer + 1]) / sum(es)
    f = lb + (1.0 - lb) * _sigmoid(f_ref[0])
    lgf = jnp.log2(f)
    kb_ref[d] = 1.0 - f
    hi = lgf.astype(BF16)
    r1 = lgf - hi.astype(F32)
    mid = r1.astype(BF16)
    lo = (r1 - mid.astype(F32)).astype(BF16)
    sums_ref[d] = _dot(p_ref[...], jnp.concatenate([hi, mid, lo], axis=0))


def _hgrn_head(q_ref, i_ref, o_ref, st_ref, kb_ref, sums_ref, h, *, d, L, nh, dk, dv):
    levels = _hgrn_levels(L)
    rowi = lax.broadcasted_iota(I32, (L, 1), 0)
    xor = lax.broadcasted_iota(I32, (L, L), 0) ^ lax.broadcasted_iota(I32, (L, L), 1)
    last = L - 1 if d == 0 else 0
    ks = slice(h * dk, (h + 1) * dk)
    vs = slice(h * dv, (h + 1) * dv)
    qh = q_ref[0, :, ks].astype(F32)
    kbh = kb_ref[d, :, ks]
    ih = i_ref[0, :, vs]
    eb = jnp.exp2(sums_ref[d, 0:L, ks])
    ea = jnp.exp2(sums_ref[d, L:2 * L, ks])
    a = jnp.zeros((L, L), F32)
    for li, c in enumerate(levels):
        k = c.bit_length() - 1
        bit = (rowi >> k) & 1
        if d == 1:
            bit = 1 - bit
        el = jnp.exp2(sums_ref[d, (2 + li) * L:(3 + li) * L, ks])
        z = (jnp.where(bit == 1, qh, kbh) * el).astype(BF16)
        pair = ((xor >> k) + ((1 - bit) << 8)) == 1
        a = jnp.where(pair, _dot_nt(z, z), a)
    st = st_ref[d * nh + h]
    o = (_dot(a.astype(BF16), ih) + _dot_nt((qh * eb).astype(BF16), st.astype(BF16))
         + jnp.sum(qh * kbh, axis=1, keepdims=True) * ih.astype(F32))
    o_ref[0, :, vs] = o
    st_ref[d * nh + h] = st * eb[last:last + 1, :] + _dot_tn(ih, (kbh * ea).astype(BF16))


def _hgrn_body(q_f, i_f, f_f, q_b, i_b, f_b, lb_ref, pf_ref, pb_ref, of_ref, ob_ref, st_ref,
               kb_ref, sums_ref, *, L, nh, dk, dv, layer):
    @pl.when(pl.program_id(1) == 0)
    def _():
        st_ref[...] = jnp.zeros_like(st_ref)

    _hgrn_prep(f_f, lb_ref, pf_ref, kb_ref, sums_ref, d=0, layer=layer)
    _hgrn_prep(f_b, lb_ref, pb_ref, kb_ref, sums_ref, d=1, layer=layer)
    kw = dict(L=L, nh=nh, dk=dk, dv=dv)

    for h in range(nh):
        _hgrn_head(q_f, i_f, of_ref, st_ref, kb_ref, sums_ref, h, d=0, **kw)
        _hgrn_head(q_b, i_b, ob_ref, st_ref, kb_ref, sums_ref, h, d=1, **kw)


def _hgrn(proj3, q_blk, i_blk, fpre, lb, layer, nh, dk, dv):
    B, S, _ = proj3.shape
    L = HGRN_CHUNK
    nc = S // L
    W = nh * dk
    pf = jnp.asarray(_hgrn_sum_matrix(L, 0), BF16)
    pb = jnp.asarray(_hgrn_sum_matrix(L, 1), BF16)
    fw = lambda b, j: (b, j)
    bw = lambda b, j: (b, nc - 1 - j)
    specs = []
    for d, im in enumerate((fw, bw)):
        specs += [pl.BlockSpec((1, L, W), (lambda b, j, im=im: im(b, j) + (q_blk,))),
                  pl.BlockSpec((1, L, nh * dv), (lambda b, j, im=im: im(b, j) + (i_blk,))),
                  pl.BlockSpec((1, L, W), (lambda b, j, im=im, d=d: im(b, j) + (d,)))]
    specs += [pl.BlockSpec(lb.shape, lambda b, j: (0, 0, 0)),
              pl.BlockSpec(pf.shape, lambda b, j: (0, 0)),
              pl.BlockSpec(pb.shape, lambda b, j: (0, 0))]
    return pl.pallas_call(
        functools.partial(_hgrn_body, L=L, nh=nh, dk=dk, dv=dv, layer=layer),
        grid=(B, nc),
        in_specs=specs,
        out_specs=[pl.BlockSpec((1, L, nh * dv), lambda b, j: (b, j, 0)),
                   pl.BlockSpec((1, L, nh * dv), lambda b, j: (b, nc - 1 - j, 0))],
        out_shape=[jax.ShapeDtypeStruct((B, S, nh * dv), F32)] * 2,
        scratch_shapes=[pltpu.VMEM((2 * nh, dv, dk), F32),
                        pltpu.VMEM((2, L, W), F32),
                        pltpu.VMEM((2, pf.shape[0], W), F32)],
        compiler_params=_cparams(("arbitrary", "arbitrary")),
        name="hgrn2_scan",
    )(proj3, proj3, fpre, proj3, proj3, fpre, lb, pf, pb)


def _head_rms(x, nh, dh):
    outs = []
    for h in range(nh):
        xh = x[:, h * dh:(h + 1) * dh]
        outs.append(xh * lax.rsqrt(jnp.mean(xh * xh, axis=-1, keepdims=True) + LN_EPS))
    return jnp.concatenate(outs, axis=1)


def _post_body(hmf, hmb, hof, hob, om, gh, gm, ghh, x_ref, l0g, l0b, mg, hg, wbm, wbh, wout, l1g, l1b, wr,
               h1_ref, aff_ref, *, nh_m, dv_m, nh_h, dv_h, alpha):
    hm = _head_rms(hmf[...] + hmb[...], nh_m, dv_m) * mg[...] * _gate_sigmoid(om[...].astype(F32))
    hgate = gh[...].astype(F32)
    ho = _head_rms(hof[...] + hob[...], nh_h, dv_h) * hg[...] * (hgate * _gate_sigmoid(hgate))
    y_m = _dot(hm.astype(BF16), wbm[...])
    y_h = _dot(ho.astype(BF16), wbh[...])
    merged = _gate_sigmoid(gm[...].astype(F32)) * y_m + _gate_sigmoid(ghh[...].astype(F32)) * y_h
    mix = _dot(merged.astype(BF16), wout[...])
    h = _layer_norm(x_ref[...], l0g[...], l0b[...])
    h1 = _layer_norm(alpha * h + mix, l1g[...], l1b[...])
    h1_ref[...] = h1
    logits = _dot_nt(wr[...], h1.astype(BF16))
    z = jnp.exp(logits - jnp.max(logits, axis=0, keepdims=True))
    aff_ref[0] = z / jnp.sum(z, axis=0, keepdims=True)


def _post_mixer(hm_f, hm_b, ho_f, ho_b, proj, cols, x2, l0g, l0b, mg, hg, wbm, wbh, wout, l1g, l1b, wr_t,
                B, S, dims, alpha, tm=256):
    T, D = x2.shape
    nh_m, dv_m, nh_h, dv_h = dims
    vm, vh = nh_m * dv_m, nh_h * dv_h
    E = wr_t.shape[0]
    nbs = S // tm
    row = lambda i: (i, 0)
    const = lambda i: (0, 0)
    in_specs = [pl.BlockSpec((tm, vm), row), pl.BlockSpec((tm, vm), row),
                pl.BlockSpec((tm, vh), row), pl.BlockSpec((tm, vh), row),
                pl.BlockSpec((tm, vm), lambda i: (i, cols["o_m"] // vm)),
                pl.BlockSpec((tm, vh), lambda i: (i, cols["g_h"] // vh)),
                pl.BlockSpec((tm, D), lambda i: (i, cols["gate_m"] // D)),
                pl.BlockSpec((tm, D), lambda i: (i, cols["gate_h"] // D)),
                pl.BlockSpec((tm, D), row), pl.BlockSpec((1, D), const), pl.BlockSpec((1, D), const),
                pl.BlockSpec((1, vm), const), pl.BlockSpec((1, vh), const),
                pl.BlockSpec((vm, D), const), pl.BlockSpec((vh, D), const), pl.BlockSpec((D, D), const),
                pl.BlockSpec((1, D), const), pl.BlockSpec((1, D), const),
                pl.BlockSpec((E, D), const)]
    return pl.pallas_call(
        functools.partial(_post_body, nh_m=nh_m, dv_m=dv_m, nh_h=nh_h, dv_h=dv_h, alpha=alpha),
        grid=(T // tm,),
        in_specs=in_specs,
        out_specs=[pl.BlockSpec((tm, D), row),
                   pl.BlockSpec((1, E, tm), lambda i: (i // nbs, 0, i % nbs))],
        out_shape=[jax.ShapeDtypeStruct((T, D), F32), jax.ShapeDtypeStruct((B, E, S), F32)],
        compiler_params=_cparams(("arbitrary",)),
        name="mixer_out_ln1_router",
    )(hm_f, hm_b, ho_f, ho_b, proj, proj, proj, proj, x2, l0g, l0b, mg, hg, wbm, wbh, wout, l1g, l1b, wr_t)


def _prefix_counts(mask2, upper, ones, bdl):
    mb = mask2.astype(BF16)
    within = _dot(mb, upper)
    rowtot = _dot(mb, ones)
    before = _dot(bdl, rowtot.astype(BF16))
    return within + before, within


def _topk_body(a_ref, upper_ref, ones_ref, bdl_ref, idx_ref, pos_ref, off_ref, *, E, R, cap):
    a3 = a_ref[0]

    def count(m):
        return jnp.sum(jnp.sum(jnp.where(m, 1.0, 0.0), axis=2, keepdims=True), axis=1, keepdims=True)

    def as_float(bits):
        return pltpu.bitcast(jnp.broadcast_to(bits, a3.shape), F32)

    def step(it, cur):
        cand = cur | (jnp.int32(1) << (30 - it))
        return jnp.where(count(a3 >= as_float(cand)) >= cap, cand, cur)

    thr = lax.fori_loop(0, 31, step, jnp.zeros((E, 1, 1), I32))
    gt = a3 >= as_float(thr + 1)
    eq = jnp.logical_and(a3 >= as_float(thr), jnp.logical_not(gt))
    need = cap - count(gt)
    upper, ones, bdl = upper_ref[...], ones_ref[...], bdl_ref[...]
    eq_incl, _ = _prefix_counts(jnp.where(eq, 1.0, 0.0).reshape(E * R, 128), upper, ones, bdl)
    sel = jnp.logical_or(gt, jnp.logical_and(eq, eq_incl.reshape(E, R, 128) <= need))
    self32 = jnp.where(sel, 1.0, 0.0).reshape(E * R, 128)
    incl, within = _prefix_counts(self32, upper, ones, bdl)
    pos_ref[0] = jnp.where(sel, incl.reshape(E, R, 128) - 1.0, -1.0).astype(I32)

    slot = lax.broadcasted_iota(I32, (1, cap), 1).astype(F32)
    rsub = lax.broadcasted_iota(I32, (R, cap), 0).astype(F32)
    for e in range(E):
        inc_e = incl[e * R:(e + 1) * R]
        row_incl = inc_e[:, 127:128]
        row_excl = row_incl - within[e * R:(e + 1) * R][:, 127:128]
        off_ref[0, e] = jnp.broadcast_to(row_excl, (R, 128)).astype(I32)
        ridx = jnp.sum(jnp.where(row_incl <= slot, 1.0, 0.0), axis=0, keepdims=True)
        onehot = rsub == ridx
        start = jnp.sum(jnp.where(onehot, row_excl, 0.0), axis=0, keepdims=True)
        local = slot - start
        pg = _dot_tn(within[e * R:(e + 1) * R].astype(BF16), jnp.where(onehot, 1.0, 0.0).astype(BF16))
        lane = jnp.sum(jnp.where(pg <= local, 1.0, 0.0), axis=0, keepdims=True)
        idx_ref[0, e:e + 1, :] = (ridx * 128.0 + lane).astype(I32)


def _topk(aff_t, cap):
    B, E, S = aff_t.shape
    R = S // 128
    k = np.arange(128)
    upper = jnp.asarray((k[:, None] <= k[None, :]).astype(np.float32), BF16)
    ones = jnp.ones((128, 128), BF16)
    r = np.arange(E * R)
    bdl = jnp.asarray(((r[:, None] // R == r[None, :] // R) & (r[None, :] < r[:, None])).astype(np.float32), BF16)
    const2 = lambda b: (0, 0)
    return pl.pallas_call(
        functools.partial(_topk_body, E=E, R=R, cap=cap),
        grid=(B,),
        in_specs=[pl.BlockSpec((1, E, R, 128), lambda b: (b, 0, 0, 0)),
                  pl.BlockSpec((128, 128), const2), pl.BlockSpec((128, 128), const2),
                  pl.BlockSpec((E * R, E * R), const2)],
        out_specs=[pl.BlockSpec((1, E, cap), lambda b: (b, 0, 0)),
                   pl.BlockSpec((1, E, R, 128), lambda b: (b, 0, 0, 0)),
                   pl.BlockSpec((1, E, R, 128), lambda b: (b, 0, 0, 0))],
        out_shape=[jax.ShapeDtypeStruct((B, E, cap), I32),
                   jax.ShapeDtypeStruct((B, E, R, 128), I32),
                   jax.ShapeDtypeStruct((B, E, R, 128), I32)],
        compiler_params=_cparams(("arbitrary",)),
        name="expert_topk",
    )(aff_t.reshape(B, E, R, 128), upper, ones, bdl)


def _ffn_body(idx_ref, h_hbm, wg_ref, wu_ref, wd_ref, ye_ref, xg, xb, acc, sem, *, E, B, C, rps):
    e = pl.program_id(0)
    b = pl.program_id(1)
    f = pl.program_id(2)
    nf = pl.num_programs(2)
    nrows = xg.shape[1]
    pair = e * B + b
    slot = pair % 2
    nxt = (pair + 1) % (E * B)
    e_n = nxt // B
    b_n = nxt % B

    def row_copy(bb, src_row, sl, dst_row):
        return pltpu.make_async_copy(h_hbm.at[bb, pl.ds(src_row, 1), :],
                                     xg.at[sl, pl.ds(dst_row, 1), :], sem.at[sl])

    def src_row(bb, ee, s):
        return idx_ref[(bb * E + ee) * C + jnp.minimum(s, C - 1)]

    def wait_all(sl):
        def wait(s, carry):
            row_copy(0, 0, sl, s).wait()
            return carry
        lax.fori_loop(0, nrows, wait, 0, unroll=rps)

    @pl.when(f == 0)
    def _():
        @pl.when(pair == 0)
        def _():
            def issue(s, carry):
                row_copy(b, src_row(b, e, s), slot, s).start()
                return carry
            lax.fori_loop(0, nrows, issue, 0)

        wait_all(slot)
        xb[...] = xg[slot, 0:C, :].astype(BF16)
        acc[...] = jnp.zeros_like(acc)

    for r in range(rps):
        s = f * rps + r
        row_copy(b_n, src_row(b_n, e_n, s), 1 - slot, s).start()

    x = xb[...]
    g = _dot(x, wg_ref[0].astype(BF16))
    u = _dot(x, wu_ref[0].astype(BF16))
    hid = (g * _gate_sigmoid(g) * u).astype(BF16)
    acc[...] += _dot(hid, wd_ref[0].astype(BF16))

    @pl.when(f == nf - 1)
    def _():
        ye_ref[0, 0] = acc[...].astype(ye_ref.dtype)

        @pl.when(pair == E * B - 1)
        def _():
            wait_all(1 - slot)


def _expert_ffn(idx, h1, w_gate, w_up, w_down, tf=256):
    B, S, D = h1.shape
    E, _, F = w_gate.shape
    C = idx.shape[-1]
    nf = F // tf
    rps = pl.cdiv(C, nf)
    grid_spec = pltpu.PrefetchScalarGridSpec(
        num_scalar_prefetch=1,
        grid=(E, B, nf),
        in_specs=[pl.BlockSpec(memory_space=pl.ANY),
                  pl.BlockSpec((1, D, tf), lambda e, b, f, idx: (e, 0, f)),
                  pl.BlockSpec((1, D, tf), lambda e, b, f, idx: (e, 0, f)),
                  pl.BlockSpec((1, tf, D), lambda e, b, f, idx: (e, f, 0))],
        out_specs=pl.BlockSpec((1, 1, C, D), lambda e, b, f, idx: (b, e, 0, 0)),
        scratch_shapes=[pltpu.VMEM((2, nf * rps, D), F32), pltpu.VMEM((C, D), BF16),
                        pltpu.VMEM((C, D), F32), pltpu.SemaphoreType.DMA((2,))],
    )
    return pl.pallas_call(
        functools.partial(_ffn_body, E=E, B=B, C=C, rps=rps),
        grid_spec=grid_spec,
        out_shape=jax.ShapeDtypeStruct((B, E, C, D), BF16),
        compiler_params=_cparams(("arbitrary", "arbitrary", "arbitrary")),
        name="expert_ffn",
    )(idx.reshape(-1), h1, w_gate, w_up, w_down)


COMBINE_ROWS = 16
COMBINE_KT = 256


def _combine_body(st_ref, h_ref, pos_ref, aff_ref, ye_hbm, g_ref, b_ref, o_ref, buf, acc, sem,
                  *, E, B, TB, NTB, alpha):
    b = pl.program_id(0)
    tb = pl.program_id(1)
    rows, kt = COMBINE_ROWS, COMBINE_KT
    step = b * NTB + tb
    slot = step % 2

    def chunk_copy(bb, e, src, sl, dst):
        return pltpu.make_async_copy(ye_hbm.at[bb, e, pl.ds(src, rows), :],
                                     buf.at[sl, pl.ds(dst, rows), :], sem.at[sl])

    def plan(bb, tt):
        koff = jnp.int32(0)
        out = []
        for e in range(E):
            base = (bb * E + e) * (NTB + 1) + tt
            ws = (st_ref[base] // rows) * rows
            n = (st_ref[base + 1] - ws + rows - 1) // rows
            out.append((ws, n, koff))
            koff = koff + n * rows
        return out, koff

    def issue_all(bb, tt, sl):
        for e, (ws, n, koff) in enumerate(plan(bb, tt)[0]):
            def issue(c, carry, e=e, ws=ws, koff=koff):
                chunk_copy(bb, e, pl.multiple_of(ws + c * rows, rows), sl,
                           pl.multiple_of(koff + c * rows, rows)).start()
                return carry
            lax.fori_loop(0, n, issue, 0)

    @pl.when(step == 0)
    def _():
        buf[...] = jnp.zeros_like(buf)
        issue_all(b, tb, slot)

    cur, ktot = plan(b, tb)

    def wait(c, carry):
        chunk_copy(0, 0, 0, slot, 0).wait()
        return carry

    lax.fori_loop(0, ktot // rows, wait, 0)

    @pl.when(step + 1 < B * NTB)
    def _():
        issue_all((step + 1) // NTB, (step + 1) % NTB, 1 - slot)

    pos = pos_ref[0]
    aff = aff_ref[0]
    keys = []
    for e, (ws, n, koff) in enumerate(cur):
        pe = pos[e:e + 1, :]
        keys.append(jnp.where(pe >= 0, pe + (koff - ws), -1))

    def onehot_t(kbase):
        ksub = lax.broadcasted_iota(I32, (kt, TB), 0) + kbase
        o = jnp.zeros((kt, TB), F32)
        for e in range(E):
            o = jnp.where(ksub == keys[e], aff[e:e + 1, :], o)
        return o.astype(BF16)

    acc[...] = _dot_tn(onehot_t(0), buf[slot, 0:kt, :])

    def kstep(kc, carry):
        kbase = pl.multiple_of(kc * kt, kt)
        acc[...] += _dot_tn(onehot_t(kbase), buf[slot, pl.ds(kbase, kt), :])
        return carry

    lax.fori_loop(1, (ktot + kt - 1) // kt, kstep, 0)
    o_ref[0] = _layer_norm(alpha * h_ref[0] + acc[...], g_ref[...], b_ref[...])


def _combine(starts, h1, pos, aff, ye, g, bb, alpha, TB):
    B, S, D = h1.shape
    E = ye.shape[1]
    NTB = S // TB
    kmax = pl.cdiv(E * (TB + 2 * COMBINE_ROWS), COMBINE_KT) * COMBINE_KT
    grid_spec = pltpu.PrefetchScalarGridSpec(
        num_scalar_prefetch=1,
        grid=(B, NTB),
        in_specs=[pl.BlockSpec((1, TB, D), lambda b, t, st: (b, t, 0)),
                  pl.BlockSpec((1, E, TB), lambda b, t, st: (b, 0, t)),
                  pl.BlockSpec((1, E, TB), lambda b, t, st: (b, 0, t)),
                  pl.BlockSpec(memory_space=pl.ANY),
                  pl.BlockSpec((1, D), lambda b, t, st: (0, 0)),
                  pl.BlockSpec((1, D), lambda b, t, st: (0, 0))],
        out_specs=pl.BlockSpec((1, TB, D), lambda b, t, st: (b, t, 0)),
        scratch_shapes=[pltpu.VMEM((2, kmax, D), BF16), pltpu.VMEM((TB, D), F32),
                        pltpu.SemaphoreType.DMA((2,))],
    )
    return pl.pallas_call(
        functools.partial(_combine_body, E=E, B=B, TB=TB, NTB=NTB, alpha=alpha),
        grid_spec=grid_spec,
        out_shape=jax.ShapeDtypeStruct((B, S, D), F32),
        compiler_params=_cparams(("arbitrary", "arbitrary")),
        name="combine_ln2",
    )(starts.reshape(-1), h1, pos, aff, ye, g.reshape(1, D), bb.reshape(1, D))


def _layer(x2, l0g, l0b, B, S, lb, layer, w_in, b_in, conv_w, mg, hg, w_bm, w_bh, w_out, l1g, l1b,
           w_router, w_gate, w_up, w_down, l2g, l2b, alpha):
    T, D = x2.shape
    nh_m, dv_m = mg.shape
    nh_h, dv_h = hg.shape
    ksz, qk2 = conv_w.shape
    dk_m = qk2 // (2 * nh_m)
    v_m = nh_m * dv_m
    v_h = nh_h * dv_h
    q_h = lb.shape[-1]
    dk_h = q_h // nh_h
    E = w_router.shape[1]
    ng = 4 * nh_m
    sizes = (qk2, v_m, v_m, ng // 2, ng // 2, q_h, 2 * q_h, v_h, v_h, D, D)
    names = ("qk_m", "v_m", "o_m", "i_m", "f_m", "q_h", "f_h", "i_h", "g_h", "gate_m", "gate_h")
    start = dict(zip(names, np.concatenate([[0], np.cumsum(sizes)[:-1]]).tolist()))
    size = dict(zip(names, sizes))
    sl = lambda a, n: a[..., start[n]:start[n] + size[n]]
    a_names = ("qk_m", "v_m", "o_m", "q_h", "i_h", "g_h", "gate_m", "gate_h")
    cols = dict(zip(a_names, np.concatenate([[0], np.cumsum([size[n] for n in a_names])[:-1]]).tolist()))
    n_a = sum(size[n] for n in a_names)
    w_af = jnp.concatenate([sl(w_in, n) for n in a_names + ("f_h",)], axis=1).astype(BF16)
    b_af = jnp.concatenate([sl(b_in, n) for n in a_names + ("f_h",)])
    gpad = 128 - ng
    w_g = jnp.pad(jnp.concatenate([sl(w_in, "i_m"), sl(w_in, "f_m")], axis=1), ((0, 0), (0, gpad))).astype(BF16)
    b_g = jnp.pad(jnp.concatenate([sl(b_in, "i_m"), sl(b_in, "f_m")]), (0, gpad))

    hb = _ln_in(x2, l0g, l0b)
    proj, fpre, gates = _proj(hb, w_af, b_af, w_g, b_g, n_a)
    gates = gates[:, :ng].reshape(B, S, ng)
    proj3 = proj.reshape(B, S, -1)

    kscale = jnp.concatenate([jnp.ones((qk2 // 2,), F32), jnp.full((qk2 // 2,), dk_m ** -0.5, F32)])
    qk = _qk_conv(proj3, conv_w, kscale.reshape(1, qk2), qk2)
    hm_f, hm_b = _mlstm(qk, proj3, cols["v_m"] // v_m, gates, jnp.swapaxes(gates, 1, 2), nh_m, dk_m, dv_m)

    ho_f, ho_b = _hgrn(proj3, cols["q_h"] // q_h, cols["i_h"] // v_h, fpre.reshape(B, S, 2 * q_h), lb,
                       layer, nh_h, dk_h, dv_h)

    h1, aff_t = _post_mixer(hm_f.reshape(T, v_m), hm_b.reshape(T, v_m), ho_f.reshape(T, v_h),
                            ho_b.reshape(T, v_h), proj, cols, x2, l0g, l0b, mg.reshape(1, v_m), hg.reshape(1, v_h),
                            w_bm.astype(BF16), w_bh.astype(BF16), w_out.astype(BF16),
                            l1g.reshape(1, D), l1b.reshape(1, D), w_router.T.astype(BF16),
                            B, S, (nh_m, dv_m, nh_h, dv_h), alpha)

    cap = CAP_FACTOR * S // E
    idx, pos, off = _topk(aff_t, cap)
    h1_3 = h1.reshape(B, S, D)
    ye = _expert_ffn(idx, h1_3, w_gate, w_up, w_down)
    TB = 256
    starts = jnp.concatenate([off[:, :, ::TB // 128, 0], jnp.full((B, E, 1), cap, I32)], axis=2)
    return _combine(starts, h1_3, pos.reshape(B, E, S), aff_t, ye, l2g, l2b, alpha, TB=TB).reshape(T, D)


def kernel(x, ln_in_g, ln_in_b, hgrn_lb_logits, w_in, b_in, conv_w, mlstm_norm_g, hgrn_norm_g,
           w_branch_m, w_branch_h, w_out, ln1_g, ln1_b, w_router, w_gate_e, w_up_e, w_down_e,
           ln2_g, ln2_b):
    B, S, D = x.shape
    depth = w_in.shape[0]
    alpha = (2.0 * depth) ** 0.25
    lb_logits = jnp.swapaxes(hgrn_lb_logits.astype(F32), 0, 1)
    assert depth == 1, "the layer-to-layer hand-over is written for a single layer"
    out = _layer(x.reshape(B * S, D), ln_in_g.reshape(1, D), ln_in_b.reshape(1, D), B, S, lb_logits, 0,
                 w_in[0], b_in[0], conv_w[0], mlstm_norm_g[0], hgrn_norm_g[0], w_branch_m[0],
                 w_branch_h[0], w_out[0], ln1_g[0], ln1_b[0], w_router[0], w_gate_e[0], w_up_e[0],
                 w_down_e[0], ln2_g[0], ln2_b[0], alpha)
    return out.reshape(B, S, D)
```

```python
import functools

import numpy as np
import jax
import jax.numpy as jnp
from jax import lax
from jax.experimental import pallas as pl
from jax.experimental.pallas import tpu as pltpu

F32 = jnp.float32
BF16 = jnp.bfloat16
I32 = jnp.int32

LN_EPS = 1e-5
CAP_FACTOR = 2
MLSTM_CHUNK = 256
HGRN_CHUNK = 64
VMEM_LIMIT = 56 * 1024 * 1024

_NT = (((1,), (1,)), ((), ()))
_TN = (((0,), (0,)), ((), ()))


def _cparams(sem):
    return pltpu.CompilerParams(dimension_semantics=sem, vmem_limit_bytes=VMEM_LIMIT)


def _dot(a, b):
    return jnp.dot(a, b, preferred_element_type=F32)


def _dot_nt(a, b):
    return lax.dot_general(a, b, _NT, preferred_element_type=F32)


def _dot_tn(a, b):
    return lax.dot_general(a, b, _TN, preferred_element_type=F32)


def _dot_exact(a, b):
    return jnp.dot(a, b, preferred_element_type=F32, precision=lax.Precision.HIGHEST)


def _sigmoid(x):
    return 1.0 / (1.0 + jnp.exp(-x))


def _gate_sigmoid(x):
    return 0.5 * jnp.tanh(0.5 * x) + 0.5


def _layer_norm(x, g, b):
    mu = jnp.mean(x, axis=-1, keepdims=True)
    xc = x - mu
    var = jnp.mean(xc * xc, axis=-1, keepdims=True)
    return xc * lax.rsqrt(var + LN_EPS) * g + b


def _ln_in_body(x_ref, g_ref, b_ref, hb_ref):
    hb_ref[...] = _layer_norm(x_ref[...], g_ref[...], b_ref[...]).astype(BF16)


def _ln_in(x2, g, b, tm=512):
    T, D = x2.shape
    return pl.pallas_call(
        _ln_in_body,
        grid=(T // tm,),
        in_specs=[pl.BlockSpec((tm, D), lambda i: (i, 0)),
                  pl.BlockSpec((1, D), lambda i: (0, 0)),
                  pl.BlockSpec((1, D), lambda i: (0, 0))],
        out_specs=pl.BlockSpec((tm, D), lambda i: (i, 0)),
        out_shape=jax.ShapeDtypeStruct((T, D), BF16),
        compiler_params=_cparams(("arbitrary",)),
        name="ln_in",
    )(x2, g, b)


def _proj_body(x_ref, w_ref, b_ref, o_ref):
    o_ref[...] = (_dot(x_ref[...], w_ref[...]) + b_ref[...]).astype(o_ref.dtype)


def _proj(xb, w, b, out_dtype, tm, tn):
    T, K = xb.shape
    N = w.shape[1]
    return pl.pallas_call(
        _proj_body,
        grid=(N // tn, T // tm),
        in_specs=[pl.BlockSpec((tm, K), lambda j, i: (i, 0)),
                  pl.BlockSpec((K, tn), lambda j, i: (0, j)),
                  pl.BlockSpec((1, tn), lambda j, i: (0, j))],
        out_specs=pl.BlockSpec((tm, tn), lambda j, i: (i, j)),
        out_shape=jax.ShapeDtypeStruct((T, N), out_dtype),
        compiler_params=_cparams(("arbitrary", "arbitrary")),
        name="in_proj",
    )(xb, w, b.reshape(1, N))


def _conv_body(prev_ref, cur_ref, nxt_ref, w_ref, s_ref, o_ref, *, tr, ksz):
    i = pl.program_id(1)
    n = pl.num_programs(1)
    keep_prev = jnp.where(i > 0, 1.0, 0.0).astype(F32)
    keep_next = jnp.where(i < n - 1, 1.0, 0.0).astype(F32)
    xp = jnp.concatenate([prev_ref[0].astype(F32) * keep_prev,
                          cur_ref[0].astype(F32),
                          nxt_ref[0].astype(F32) * keep_next], axis=0)
    rows = tr + 16
    pad = ksz // 2
    acc = jnp.zeros((tr, xp.shape[1]), F32)
    for j in range(ksz):
        d = j - pad
        sh = xp if d == 0 else pltpu.roll(xp, (-d) % rows, 0)
        acc = acc + w_ref[j:j + 1, :] * sh[8:8 + tr, :]
    y = acc * _gate_sigmoid(acc)
    o_ref[0] = (y * s_ref[...]).astype(o_ref.dtype)


def _qk_conv(proj3, conv_w, scale, C, tr=512):
    B, S, _ = proj3.shape
    ksz = conv_w.shape[0]
    hb = tr // 8
    nhb = S // 8
    return pl.pallas_call(
        functools.partial(_conv_body, tr=tr, ksz=ksz),
        grid=(B, S // tr),
        in_specs=[pl.BlockSpec((1, 8, C), lambda b, i: (b, jnp.maximum(i * hb - 1, 0), 0)),
                  pl.BlockSpec((1, tr, C), lambda b, i: (b, i, 0)),
                  pl.BlockSpec((1, 8, C), lambda b, i: (b, jnp.minimum((i + 1) * hb, nhb - 1), 0)),
                  pl.BlockSpec((ksz, C), lambda b, i: (0, 0)),
                  pl.BlockSpec((1, C), lambda b, i: (0, 0))],
        out_specs=pl.BlockSpec((1, tr, C), lambda b, i: (b, i, 0)),
        out_shape=jax.ShapeDtypeStruct((B, S, C), BF16),
        compiler_params=_cparams(("arbitrary", "arbitrary")),
        name="qk_conv",
    )(proj3, proj3, proj3, conv_w, scale)


def _log_sigmoid(x):
    return jnp.minimum(x, 0.0) - jnp.log1p(jnp.exp(-jnp.abs(x)))


def _mlstm_stage1(g_ref, gt_ref, e_ref, rows_ref, slot, *, d, L, nh, dk):
    row = lax.broadcasted_iota(I32, (L, L), 0)
    col = lax.broadcasted_iota(I32, (L, L), 1)
    if d == 0:
        causal = col <= row
        causal_t = col >= row
        last = L - 1
    else:
        causal = col >= row
        causal_t = col <= row
        last = 0
    gcols = g_ref[0]
    grows = gt_ref[0]
    cs_col = _dot_exact(jnp.where(causal, 1.0, 0.0).astype(F32), _log_sigmoid(gcols))
    cs_row = _dot_exact(_log_sigmoid(grows), jnp.where(causal_t, 1.0, 0.0).astype(F32))
    for h in range(nh):
        ci = d * nh + h
        cf = 2 * nh + d * nh + h
        sidx = d * nh + h
        gc = cs_col[:, cf:cf + 1]
        dm = jnp.where(causal, gc - cs_row[cf:cf + 1, :] + grows[ci:ci + 1, :], -jnp.inf)
        rm = jnp.max(dm, axis=1, keepdims=True)
        e_ref[slot, sidx] = jnp.exp(dm - rm)
        rows_ref[slot, sidx, 0] = jnp.broadcast_to(rm, (L, dk))
        rows_ref[slot, sidx, 1] = jnp.broadcast_to(gc, (L, dk))
        rows_ref[slot, sidx, 2] = jnp.broadcast_to(gc[last:last + 1, :] - gc + gcols[:, ci:ci + 1], (L, dk))


def _mlstm_stage2(qk_ref, v_ref, o_ref, c_ref, n_ref, m_ref, e_ref, rows_ref, slot, *, d, L, nh, dk, dv):
    last = L - 1 if d == 0 else 0
    rep = dv // dk
    wide = lambda x: jnp.concatenate([x] * rep, axis=1)
    ones = jnp.ones((L, dk), BF16)
    qk = qk_ref[0]
    v_all = v_ref[0]
    for h in range(nh):
        sidx = d * nh + h
        rm = rows_ref[slot, sidx, 0]
        gc = rows_ref[slot, sidx, 1]
        mprev = m_ref[sidx][0:1, :]
        inter = gc + mprev
        mt = jnp.maximum(rm, inter)
        corr = jnp.exp(rm - mt)
        sc = jnp.exp(inter - mt)
        q = qk[:, h * dk:(h + 1) * dk]
        k = qk[:, (nh + h) * dk:(nh + h + 1) * dk]
        v = v_all[:, h * dv:(h + 1) * dv]
        p = (e_ref[slot, sidx] * _dot_nt(q, k)).astype(BF16)
        cst = c_ref[sidx]
        nst = n_ref[sidx]
        num = wide(corr) * _dot(p, v) + wide(sc) * _dot(q, cst.astype(BF16))
        qn = _dot_nt(q, jnp.concatenate([nst] * (dk // 8), axis=0).astype(BF16))
        den = corr * _dot(p, ones) + sc * qn
        o_ref[0, :, h * dv:(h + 1) * dv] = num / wide(jnp.maximum(jnp.abs(den), jnp.exp(-mt)))
        mnew = mt[last:last + 1, :]
        wk = jnp.exp(rows_ref[slot, sidx, 2] - mnew)
        decay = jnp.exp(gc[last:last + 1, :] + mprev - mnew)
        kw = k.astype(F32) * wk
        c_ref[sidx] = wide(decay) * cst + _dot_tn(kw.astype(BF16), v)
        n_ref[sidx] = jnp.broadcast_to(decay * nst[0:1, :] + jnp.sum(kw, axis=0, keepdims=True), (8, dk))
        m_ref[sidx] = jnp.broadcast_to(mnew, (8, dk))


def _mlstm_body(g_f, gt_f, g_b, gt_b, qk_f, v_f, qk_b, v_b, of_ref, ob_ref, c_ref, n_ref, m_ref,
                e_ref, rows_ref, *, L, nh, dk, dv):
    j = pl.program_id(1)

    @pl.when(j == 0)
    def _():
        e_ref[1] = jnp.zeros(e_ref.shape[1:], F32)
        rows_ref[1] = jnp.zeros(rows_ref.shape[1:], F32)

    @pl.when(j <= 1)
    def _():
        c_ref[...] = jnp.zeros_like(c_ref)
        n_ref[...] = jnp.zeros_like(n_ref)
        m_ref[...] = jnp.zeros_like(m_ref)

    def step(cur):
        for d, (g, gt) in enumerate(((g_f, gt_f), (g_b, gt_b))):
            _mlstm_stage1(g, gt, e_ref, rows_ref, cur, d=d, L=L, nh=nh, dk=dk)
        for d, (qk, v, o) in enumerate(((qk_f, v_f, of_ref), (qk_b, v_b, ob_ref))):
            _mlstm_stage2(qk, v, o, c_ref, n_ref, m_ref, e_ref, rows_ref, 1 - cur,
                          d=d, L=L, nh=nh, dk=dk, dv=dv)

    for parity in (0, 1):
        pl.when(j % 2 == parity)(functools.partial(step, parity))


def _mlstm(qk, proj3, v_blk, gates, gates_t, nh, dk, dv):
    B, S, _ = qk.shape
    L = MLSTM_CHUNK
    nc = S // L
    G = gates.shape[-1]
    assert dv % dk == 0 and dk % 8 == 0
    prep_f = lambda b, j: (b, jnp.minimum(j, nc - 1))
    prep_b = lambda b, j: (b, jnp.maximum(nc - 1 - j, 0))
    run_f = lambda b, j: (b, jnp.maximum(j - 1, 0))
    run_b = lambda b, j: (b, jnp.minimum(nc - j, nc - 1))
    specs = []
    for im in (prep_f, prep_b):
        specs += [pl.BlockSpec((1, L, G), (lambda b, j, im=im: im(b, j) + (0,))),
                  pl.BlockSpec((1, G, L), (lambda b, j, im=im: (im(b, j)[0], 0, im(b, j)[1])))]
    for im in (run_f, run_b):
        specs += [pl.BlockSpec((1, L, 2 * nh * dk), (lambda b, j, im=im: im(b, j) + (0,))),
                  pl.BlockSpec((1, L, nh * dv), (lambda b, j, im=im: im(b, j) + (v_blk,)))]
    return pl.pallas_call(
        functools.partial(_mlstm_body, L=L, nh=nh, dk=dk, dv=dv),
        grid=(B, nc + 1),
        in_specs=specs,
        out_specs=[pl.BlockSpec((1, L, nh * dv), lambda b, j: run_f(b, j) + (0,)),
                   pl.BlockSpec((1, L, nh * dv), lambda b, j: run_b(b, j) + (0,))],
        out_shape=[jax.ShapeDtypeStruct((B, S, nh * dv), F32)] * 2,
        scratch_shapes=[pltpu.VMEM((2 * nh, dk, dv), F32),
                        pltpu.VMEM((2 * nh, 8, dk), F32),
                        pltpu.VMEM((2 * nh, 8, dk), F32),
                        pltpu.VMEM((2, 2 * nh, L, L), F32),
                        pltpu.VMEM((2, 2 * nh, 3, L, dk), F32)],
        compiler_params=_cparams(("arbitrary", "arbitrary")),
        name="mlstm_scan",
    )(gates, gates_t, gates, gates_t, qk, proj3, qk, proj3)


def _hgrn_levels(L):
    c = L // 2
    out = []
    while c >= 1:
        out.append(c)
        c //= 2
    return out


def _hgrn_sum_matrix(L, d):
    t = np.arange(L)[:, None]
    r = np.arange(L)[None, :]
    if d == 0:
        blocks = [(r <= t), (r > t)]
    else:
        blocks = [(r >= t), (r < t)]
    for c in _hgrn_levels(L):
        base = (t // (2 * c)) * (2 * c)
        mid = base + c
        second = (t - base) >= c
        if d == 0:
            m = np.where(second, (r >= mid) & (r <= t), (r > t) & (r < mid))
        else:
            m = np.where(second, (r >= mid) & (r < t), (r >= t) & (r < mid))
        blocks.append(m)
    p = np.concatenate(blocks, axis=0).astype(np.float32)
    return np.concatenate([p, p, p], axis=1)


def _hgrn_prep(f_ref, lb_ref, p_ref, kb_ref, sums_ref, *, d, layer):
    slots = [lb_ref[k][d:d + 1, :] for k in range(lb_ref.shape[0])]
    top = functools.reduce(jnp.maximum, slots)
    es = [jnp.exp(s - top) for s in slots]
    lb = sum(es[:layer + 1]) / sum(es)
    f = lb + (1.0 - lb) * _sigmoid(f_ref[0])
    lgf = jnp.log2(f)
    kb_ref[d] = 1.0 - f
    hi = lgf.astype(BF16)
    r1 = lgf - hi.astype(F32)
    mid = r1.astype(BF16)
    lo = (r1 - mid.astype(F32)).astype(BF16)
    sums_ref[d] = _dot(p_ref[...], jnp.concatenate([hi, mid, lo], axis=0))


def _hgrn_head(q_ref, i_ref, o_ref, st_ref, kb_ref, sums_ref, h, *, d, L, nh, dk, dv):
    levels = _hgrn_levels(L)
    rowi = lax.broadcasted_iota(I32, (L, 1), 0)
    xor = lax.broadcasted_iota(I32, (L, L), 0) ^ lax.broadcasted_iota(I32, (L, L), 1)
    last = L - 1 if d == 0 else 0
    ks = slice(h * dk, (h + 1) * dk)
    vs = slice(h * dv, (h + 1) * dv)
    qh = q_ref[0, :, ks].astype(F32)
    kbh = kb_ref[d, :, ks]
    ih = i_ref[0, :, vs]
    eb = jnp.exp2(sums_ref[d, 0:L, ks])
    ea = jnp.exp2(sums_ref[d, L:2 * L, ks])
    a = jnp.zeros((L, L), F32)
    for li, c in enumerate(levels):
        k = c.bit_length() - 1
        bit = (rowi >> k) & 1
        if d == 1:
            bit = 1 - bit
        el = jnp.exp2(sums_ref[d, (2 + li) * L:(3 + li) * L, ks])
        z = (jnp.where(bit == 1, qh, kbh) * el).astype(BF16)
        pair = ((xor >> k) + ((1 - bit) << 8)) == 1
        a = jnp.where(pair, _dot_nt(z, z), a)
    st = st_ref[d * nh + h]
    o = (_dot(a.astype(BF16), ih) + _dot_nt((qh * eb).astype(BF16), st.astype(BF16))
         + jnp.sum(qh * kbh, axis=1, keepdims=True) * ih.astype(F32))
    o_ref[0, :, vs] = o
    st_ref[d * nh + h] = st * eb[last:last + 1, :] + _dot_tn(ih, (kbh * ea).astype(BF16))


def _hgrn_body(q_f, i_f, f_f, q_b, i_b, f_b, lb_ref, pf_ref, pb_ref, of_ref, ob_ref, st_ref,
               kb_ref, sums_ref, *, L, nh, dk, dv, layer):
    @pl.when(pl.program_id(1) == 0)
    def _():
        st_ref[...] = jnp.zeros_like(st_ref)

    _hgrn_prep(f_f, lb_ref, pf_ref, kb_ref, sums_ref, d=0, layer=layer)
    _hgrn_prep(f_b, lb_ref, pb_ref, kb_ref, sums_ref, d=1, layer=layer)
    kw = dict(L=L, nh=nh, dk=dk, dv=dv)

    for h in range(nh):
        _hgrn_head(q_f, i_f, of_ref, st_ref, kb_ref, sums_ref, h, d=0, **kw)
        _hgrn_head(q_b, i_b, ob_ref, st_ref, kb_ref, sums_ref, h, d=1, **kw)


def _hgrn(proj3, q_blk, i_blk, fpre, lb, layer, nh, dk, dv):
    B, S, _ = proj3.shape
    L = HGRN_CHUNK
    nc = S // L
    W = nh * dk
    pf = jnp.asarray(_hgrn_sum_matrix(L, 0), BF16)
    pb = jnp.asarray(_hgrn_sum_matrix(L, 1), BF16)
    fw = lambda b, j: (b, j)
    bw = lambda b, j: (b, nc - 1 - j)
    specs = []
    for d, im in enumerate((fw, bw)):
        specs += [pl.BlockSpec((1, L, W), (lambda b, j, im=im: im(b, j) + (q_blk,))),
                  pl.BlockSpec((1, L, nh * dv), (lambda b, j, im=im: im(b, j) + (i_blk,))),
                  pl.BlockSpec((1, L, W), (lambda b, j, im=im, d=d: im(b, j) + (d,)))]
    specs += [pl.BlockSpec(lb.shape, lambda b, j: (0, 0, 0)),
              pl.BlockSpec(pf.shape, lambda b, j: (0, 0)),
              pl.BlockSpec(pb.shape, lambda b, j: (0, 0))]
    return pl.pallas_call(
        functools.partial(_hgrn_body, L=L, nh=nh, dk=dk, dv=dv, layer=layer),
        grid=(B, nc),
        in_specs=specs,
        out_specs=[pl.BlockSpec((1, L, nh * dv), lambda b, j: (b, j, 0)),
                   pl.BlockSpec((1, L, nh * dv), lambda b, j: (b, nc - 1 - j, 0))],
        out_shape=[jax.ShapeDtypeStruct((B, S, nh * dv), F32)] * 2,
        scratch_shapes=[pltpu.VMEM((2 * nh, dv, dk), F32),
                        pltpu.VMEM((2, L, W), F32),
                        pltpu.VMEM((2, pf.shape[0], W), F32)],
        compiler_params=_cparams(("arbitrary", "arbitrary")),
        name="hgrn2_scan",
    )(proj3, proj3, fpre, proj3, proj3, fpre, lb, pf, pb)


def _head_rms(x, nh, dh):
    outs = []
    for h in range(nh):
        xh = x[:, h * dh:(h + 1) * dh]
        outs.append(xh * lax.rsqrt(jnp.mean(xh * xh, axis=-1, keepdims=True) + LN_EPS))
    return jnp.concatenate(outs, axis=1)


def _post_body(hmf, hmb, hof, hob, om, gh, gm, ghh, x_ref, l0g, l0b, mg, hg, wbm, wbh, wout, l1g, l1b, wr,
               h1_ref, aff_ref, *, nh_m, dv_m, nh_h, dv_h, alpha):
    hm = _head_rms(hmf[...] + hmb[...], nh_m, dv_m) * mg[...] * _gate_sigmoid(om[...].astype(F32))
    hgate = gh[...].astype(F32)
    ho = _head_rms(hof[...] + hob[...], nh_h, dv_h) * hg[...] * (hgate * _gate_sigmoid(hgate))
    y_m = _dot(hm.astype(BF16), wbm[...])
    y_h = _dot(ho.astype(BF16), wbh[...])
    merged = _gate_sigmoid(gm[...].astype(F32)) * y_m + _gate_sigmoid(ghh[...].astype(F32)) * y_h
    mix = _dot(merged.astype(BF16), wout[...])
    h = _layer_norm(x_ref[...], l0g[...], l0b[...])
    h1 = _layer_norm(alpha * h + mix, l1g[...], l1b[...])
    h1_ref[...] = h1
    logits = _dot_nt(wr[...], h1.astype(BF16))
    z = jnp.exp(logits - jnp.max(logits, axis=0, keepdims=True))
    aff_ref[0] = z / jnp.sum(z, axis=0, keepdims=True)


def _post_mixer(hm_f, hm_b, ho_f, ho_b, proj, cols, x2, l0g, l0b, mg, hg, wbm, wbh, wout, l1g, l1b, wr_t,
                B, S, dims, alpha, tm=256):
    T, D = x2.shape
    nh_m, dv_m, nh_h, dv_h = dims
    vm, vh = nh_m * dv_m, nh_h * dv_h
    E = wr_t.shape[0]
    nbs = S // tm
    row = lambda i: (i, 0)
    const = lambda i: (0, 0)
    in_specs = [pl.BlockSpec((tm, vm), row), pl.BlockSpec((tm, vm), row),
                pl.BlockSpec((tm, vh), row), pl.BlockSpec((tm, vh), row),
                pl.BlockSpec((tm, vm), lambda i: (i, cols["o_m"] // vm)),
                pl.BlockSpec((tm, vh), lambda i: (i, cols["g_h"] // vh)),
                pl.BlockSpec((tm, D), lambda i: (i, cols["gate_m"] // D)),
                pl.BlockSpec((tm, D), lambda i: (i, cols["gate_h"] // D)),
                pl.BlockSpec((tm, D), row), pl.BlockSpec((1, D), const), pl.BlockSpec((1, D), const),
                pl.BlockSpec((1, vm), const), pl.BlockSpec((1, vh), const),
                pl.BlockSpec((vm, D), const), pl.BlockSpec((vh, D), const), pl.BlockSpec((D, D), const),
                pl.BlockSpec((1, D), const), pl.BlockSpec((1, D), const),
                pl.BlockSpec((E, D), const)]
    return pl.pallas_call(
        functools.partial(_post_body, nh_m=nh_m, dv_m=dv_m, nh_h=nh_h, dv_h=dv_h, alpha=alpha),
        grid=(T // tm,),
        in_specs=in_specs,
        out_specs=[pl.BlockSpec((tm, D), row),
                   pl.BlockSpec((1, E, tm), lambda i: (i // nbs, 0, i % nbs))],
        out_shape=[jax.ShapeDtypeStruct((T, D), F32), jax.ShapeDtypeStruct((B, E, S), F32)],
        compiler_params=_cparams(("arbitrary",)),
        name="mixer_out_ln1_router",
    )(hm_f, hm_b, ho_f, ho_b, proj, proj, proj, proj, x2, l0g, l0b, mg, hg, wbm, wbh, wout, l1g, l1b, wr_t)


def _prefix_counts(mask2, upper, ones, bdl):
    mb = mask2.astype(BF16)
    within = _dot(mb, upper)
    rowtot = _dot(mb, ones)
    before = _dot(bdl, rowtot.astype(BF16))
    return within + before, within


def _topk_body(a_ref, upper_ref, ones_ref, bdl_ref, idx_ref, pos_ref, off_ref, *, E, R, cap):
    a3 = a_ref[0]

    def count(m):
        return jnp.sum(jnp.sum(jnp.where(m, 1.0, 0.0), axis=2, keepdims=True), axis=1, keepdims=True)

    def as_float(bits):
        return pltpu.bitcast(jnp.broadcast_to(bits, a3.shape), F32)

    def step(it, cur):
        cand = cur | (jnp.int32(1) << (30 - it))
        return jnp.where(count(a3 >= as_float(cand)) >= cap, cand, cur)

    thr = lax.fori_loop(0, 31, step, jnp.zeros((E, 1, 1), I32))
    gt = a3 >= as_float(thr + 1)
    eq = jnp.logical_and(a3 >= as_float(thr), jnp.logical_not(gt))
    need = cap - count(gt)
    upper, ones, bdl = upper_ref[...], ones_ref[...], bdl_ref[...]
    eq_incl, _ = _prefix_counts(jnp.where(eq, 1.0, 0.0).reshape(E * R, 128), upper, ones, bdl)
    sel = jnp.logical_or(gt, jnp.logical_and(eq, eq_incl.reshape(E, R, 128) <= need))
    self32 = jnp.where(sel, 1.0, 0.0).reshape(E * R, 128)
    incl, within = _prefix_counts(self32, upper, ones, bdl)
    pos_ref[0] = jnp.where(sel, incl.reshape(E, R, 128) - 1.0, -1.0).astype(I32)

    slot = lax.broadcasted_iota(I32, (1, cap), 1).astype(F32)
    rsub = lax.broadcasted_iota(I32, (R, cap), 0).astype(F32)
    for e in range(E):
        inc_e = incl[e * R:(e + 1) * R]
        row_incl = inc_e[:, 127:128]
        row_excl = row_incl - within[e * R:(e + 1) * R][:, 127:128]
        off_ref[0, e] = jnp.broadcast_to(row_excl, (R, 128)).astype(I32)
        ridx = jnp.sum(jnp.where(row_incl <= slot, 1.0, 0.0), axis=0, keepdims=True)
        onehot = rsub == ridx
        start = jnp.sum(jnp.where(onehot, row_excl, 0.0), axis=0, keepdims=True)
        local = slot - start
        pg = _dot_tn(within[e * R:(e + 1) * R].astype(BF16), jnp.where(onehot, 1.0, 0.0).astype(BF16))
        lane = jnp.sum(jnp.where(pg <= local, 1.0, 0.0), axis=0, keepdims=True)
        idx_ref[0, e:e + 1, :] = (ridx * 128.0 + lane).astype(I32)


def _topk(aff_t, cap):
    B, E, S = aff_t.shape
    R = S // 128
    k = np.arange(128)
    upper = jnp.asarray((k[:, None] <= k[None, :]).astype(np.float32), BF16)
    ones = jnp.ones((128, 128), BF16)
    r = np.arange(E * R)
    bdl = jnp.asarray(((r[:, None] // R == r[None, :] // R) & (r[None, :] < r[:, None])).astype(np.float32), BF16)
    const2 = lambda b: (0, 0)
    return pl.pallas_call(
        functools.partial(_topk_body, E=E, R=R, cap=cap),
        grid=(B,),
        in_specs=[pl.BlockSpec((1, E, R, 128), lambda b: (b, 0, 0, 0)),
                  pl.BlockSpec((128, 128), const2), pl.BlockSpec((128, 128), const2),
                  pl.BlockSpec((E * R, E * R), const2)],
        out_specs=[pl.BlockSpec((1, E, cap), lambda b: (b, 0, 0)),
                   pl.BlockSpec((1, E, R, 128), lambda b: (b, 0, 0, 0)),
                   pl.BlockSpec((1, E, R, 128), lambda b: (b, 0, 0, 0))],
        out_shape=[jax.ShapeDtypeStruct((B, E, cap), I32),
                   jax.ShapeDtypeStruct((B, E, R, 128), I32),
                   jax.ShapeDtypeStruct((B, E, R, 128), I32)],
        compiler_params=_cparams(("arbitrary",)),
        name="expert_topk",
    )(aff_t.reshape(B, E, R, 128), upper, ones, bdl)


def _ffn_body(idx_ref, h_hbm, wg_ref, wu_ref, wd_ref, ye_ref, xg, xb, acc, sem, *, E, B, C, rps):
    e = pl.program_id(0)
    b = pl.program_id(1)
    f = pl.program_id(2)
    nf = pl.num_programs(2)
    nrows = xg.shape[1]
    pair = e * B + b
    slot = pair % 2
    nxt = (pair + 1) % (E * B)
    e_n = nxt // B
    b_n = nxt % B

    def row_copy(bb, src_row, sl, dst_row):
        return pltpu.make_async_copy(h_hbm.at[bb, pl.ds(src_row, 1), :],
                                     xg.at[sl, pl.ds(dst_row, 1), :], sem.at[sl])

    def src_row(bb, ee, s):
        return idx_ref[(bb * E + ee) * C + jnp.minimum(s, C - 1)]

    def wait_all(sl):
        def wait(s, carry):
            row_copy(0, 0, sl, s).wait()
            return carry
        lax.fori_loop(0, nrows, wait, 0, unroll=rps)

    @pl.when(f == 0)
    def _():
        @pl.when(pair == 0)
        def _():
            def issue(s, carry):
                row_copy(b, src_row(b, e, s), slot, s).start()
                return carry
            lax.fori_loop(0, nrows, issue, 0)

        wait_all(slot)
        xb[...] = xg[slot, 0:C, :].astype(BF16)
        acc[...] = jnp.zeros_like(acc)

    for r in range(rps):
        s = f * rps + r
        row_copy(b_n, src_row(b_n, e_n, s), 1 - slot, s).start()

    x = xb[...]
    g = _dot(x, wg_ref[0].astype(BF16))
    u = _dot(x, wu_ref[0].astype(BF16))
    hid = (g * _gate_sigmoid(g) * u).astype(BF16)
    acc[...] += _dot(hid, wd_ref[0].astype(BF16))

    @pl.when(f == nf - 1)
    def _():
        ye_ref[0, 0] = acc[...].astype(ye_ref.dtype)

        @pl.when(pair == E * B - 1)
        def _():
            wait_all(1 - slot)


def _expert_ffn(idx, h1, w_gate, w_up, w_down, tf=256):
    B, S, D = h1.shape
    E, _, F = w_gate.shape
    C = idx.shape[-1]
    nf = F // tf
    rps = pl.cdiv(C, nf)
    grid_spec = pltpu.PrefetchScalarGridSpec(
        num_scalar_prefetch=1,
        grid=(E, B, nf),
        in_specs=[pl.BlockSpec(memory_space=pl.ANY),
                  pl.BlockSpec((1, D, tf), lambda e, b, f, idx: (e, 0, f)),
                  pl.BlockSpec((1, D, tf), lambda e, b, f, idx: (e, 0, f)),
                  pl.BlockSpec((1, tf, D), lambda e, b, f, idx: (e, f, 0))],
        out_specs=pl.BlockSpec((1, 1, C, D), lambda e, b, f, idx: (b, e, 0, 0)),
        scratch_shapes=[pltpu.VMEM((2, nf * rps, D), F32), pltpu.VMEM((C, D), BF16),
                        pltpu.VMEM((C, D), F32), pltpu.SemaphoreType.DMA((2,))],
    )
    return pl.pallas_call(
        functools.partial(_ffn_body, E=E, B=B, C=C, rps=rps),
        grid_spec=grid_spec,
        out_shape=jax.ShapeDtypeStruct((B, E, C, D), BF16),
        compiler_params=_cparams(("arbitrary", "arbitrary", "arbitrary")),
        name="expert_ffn",
    )(idx.reshape(-1), h1, w_gate, w_up, w_down)


COMBINE_ROWS = 16
COMBINE_KT = 256


def _combine_body(st_ref, h_ref, pos_ref, aff_ref, ye_hbm, g_ref, b_ref, o_ref, buf, acc, sem,
                  *, E, B, TB, NTB, alpha):
    b = pl.program_id(0)
    tb = pl.program_id(1)
    rows, kt = COMBINE_ROWS, COMBINE_KT
    step = b * NTB + tb
    slot = step % 2

    def chunk_copy(bb, e, src, sl, dst):
        return pltpu.make_async_copy(ye_hbm.at[bb, e, pl.ds(src, rows), :],
                                     buf.at[sl, pl.ds(dst, rows), :], sem.at[sl])

    def plan(bb, tt):
        koff = jnp.int32(0)
        out = []
        for e in range(E):
            base = (bb * E + e) * (NTB + 1) + tt
            ws = (st_ref[base] // rows) * rows
            n = (st_ref[base + 1] - ws + rows - 1) // rows
            out.append((ws, n, koff))
            koff = koff + n * rows
        return out, koff

    def issue_all(bb, tt, sl):
        for e, (ws, n, koff) in enumerate(plan(bb, tt)[0]):
            def issue(c, carry, e=e, ws=ws, koff=koff):
                chunk_copy(bb, e, pl.multiple_of(ws + c * rows, rows), sl,
                           pl.multiple_of(koff + c * rows, rows)).start()
                return carry
            lax.fori_loop(0, n, issue, 0)

    @pl.when(step == 0)
    def _():
        buf[...] = jnp.zeros_like(buf)
        issue_all(b, tb, slot)

    cur, ktot = plan(b, tb)

    def wait(c, carry):
        chunk_copy(0, 0, 0, slot, 0).wait()
        return carry

    lax.fori_loop(0, ktot // rows, wait, 0)

    @pl.when(step + 1 < B * NTB)
    def _():
        issue_all((step + 1) // NTB, (step + 1) % NTB, 1 - slot)

    pos = pos_ref[0]
    aff = aff_ref[0]
    keys = []
    for e, (ws, n, koff) in enumerate(cur):
        pe = pos[e:e + 1, :]
        keys.append(jnp.where(pe >= 0, pe + (koff - ws), -1))

    def onehot_t(kbase):
        ksub = lax.broadcasted_iota(I32, (kt, TB), 0) + kbase
        o = jnp.zeros((kt, TB), F32)
        for e in range(E):
            o = jnp.where(ksub == keys[e], aff[e:e + 1, :], o)
        return o.astype(BF16)

    acc[...] = _dot_tn(onehot_t(0), buf[slot, 0:kt, :])

    def kstep(kc, carry):
        kbase = pl.multiple_of(kc * kt, kt)
        acc[...] += _dot_tn(onehot_t(kbase), buf[slot, pl.ds(kbase, kt), :])
        return carry

    lax.fori_loop(1, (ktot + kt - 1) // kt, kstep, 0)
    o_ref[0] = _layer_norm(alpha * h_ref[0] + acc[...], g_ref[...], b_ref[...])


def _combine(starts, h1, pos, aff, ye, g, bb, alpha, TB):
    B, S, D = h1.shape
    E = ye.shape[1]
    NTB = S // TB
    kmax = pl.cdiv(E * (TB + 2 * COMBINE_ROWS), COMBINE_KT) * COMBINE_KT
    grid_spec = pltpu.PrefetchScalarGridSpec(
        num_scalar_prefetch=1,
        grid=(B, NTB),
        in_specs=[pl.BlockSpec((1, TB, D), lambda b, t, st: (b, t, 0)),
                  pl.BlockSpec((1, E, TB), lambda b, t, st: (b, 0, t)),
                  pl.BlockSpec((1, E, TB), lambda b, t, st: (b, 0, t)),
                  pl.BlockSpec(memory_space=pl.ANY),
                  pl.BlockSpec((1, D), lambda b, t, st: (0, 0)),
                  pl.BlockSpec((1, D), lambda b, t, st: (0, 0))],
        out_specs=pl.BlockSpec((1, TB, D), lambda b, t, st: (b, t, 0)),
        scratch_shapes=[pltpu.VMEM((2, kmax, D), BF16), pltpu.VMEM((TB, D), F32),
                        pltpu.SemaphoreType.DMA((2,))],
    )
    return pl.pallas_call(
        functools.partial(_combine_body, E=E, B=B, TB=TB, NTB=NTB, alpha=alpha),
        grid_spec=grid_spec,
        out_shape=jax.ShapeDtypeStruct((B, S, D), F32),
        compiler_params=_cparams(("arbitrary", "arbitrary")),
        name="combine_ln2",
    )(starts.reshape(-1), h1, pos, aff, ye, g.reshape(1, D), bb.reshape(1, D))


def _layer(x2, l0g, l0b, B, S, lb, layer, w_in, b_in, conv_w, mg, hg, w_bm, w_bh, w_out, l1g, l1b,
           w_router, w_gate, w_up, w_down, l2g, l2b, alpha):
    T, D = x2.shape
    nh_m, dv_m = mg.shape
    nh_h, dv_h = hg.shape
    ksz, qk2 = conv_w.shape
    dk_m = qk2 // (2 * nh_m)
    v_m = nh_m * dv_m
    v_h = nh_h * dv_h
    q_h = lb.shape[-1]
    dk_h = q_h // nh_h
    E = w_router.shape[1]
    ng = 4 * nh_m
    sizes = (qk2, v_m, v_m, ng // 2, ng // 2, q_h, 2 * q_h, v_h, v_h, D, D)
    names = ("qk_m", "v_m", "o_m", "i_m", "f_m", "q_h", "f_h", "i_h", "g_h", "gate_m", "gate_h")
    start = dict(zip(names, np.concatenate([[0], np.cumsum(sizes)[:-1]]).tolist()))
    size = dict(zip(names, sizes))
    sl = lambda a, n: a[..., start[n]:start[n] + size[n]]
    a_names = ("qk_m", "v_m", "o_m", "q_h", "i_h", "g_h", "gate_m", "gate_h")
    w_a = jnp.concatenate([sl(w_in, n) for n in a_names], axis=1).astype(BF16)
    b_a = jnp.concatenate([sl(b_in, n) for n in a_names])
    cols = dict(zip(a_names, np.concatenate([[0], np.cumsum([size[n] for n in a_names])[:-1]]).tolist()))
    w_f = sl(w_in, "f_h").astype(BF16)
    b_f = sl(b_in, "f_h")
    gpad = 128 - ng
    w_g = jnp.pad(jnp.concatenate([sl(w_in, "i_m"), sl(w_in, "f_m")], axis=1), ((0, 0), (0, gpad))).astype(BF16)
    b_g = jnp.pad(jnp.concatenate([sl(b_in, "i_m"), sl(b_in, "f_m")]), (0, gpad))

    hb = _ln_in(x2, l0g, l0b)
    proj = _proj(hb, w_a, b_a, BF16, tm=1024, tn=1024)
    fpre = _proj(hb, w_f, b_f, F32, tm=1024, tn=1024)
    gates = _proj(hb, w_g, b_g, F32, tm=1024, tn=128)[:, :ng].reshape(B, S, ng)
    proj3 = proj.reshape(B, S, -1)

    kscale = jnp.concatenate([jnp.ones((qk2 // 2,), F32), jnp.full((qk2 // 2,), dk_m ** -0.5, F32)])
    qk = _qk_conv(proj3, conv_w, kscale.reshape(1, qk2), qk2)
    hm_f, hm_b = _mlstm(qk, proj3, cols["v_m"] // v_m, gates, jnp.swapaxes(gates, 1, 2), nh_m, dk_m, dv_m)

    ho_f, ho_b = _hgrn(proj3, cols["q_h"] // q_h, cols["i_h"] // v_h, fpre.reshape(B, S, 2 * q_h), lb,
                       layer, nh_h, dk_h, dv_h)

    h1, aff_t = _post_mixer(hm_f.reshape(T, v_m), hm_b.reshape(T, v_m), ho_f.reshape(T, v_h),
                            ho_b.reshape(T, v_h), proj, cols, x2, l0g, l0b, mg.reshape(1, v_m), hg.reshape(1, v_h),
                            w_bm.astype(BF16), w_bh.astype(BF16), w_out.astype(BF16),
                            l1g.reshape(1, D), l1b.reshape(1, D), w_router.T.astype(BF16),
                            B, S, (nh_m, dv_m, nh_h, dv_h), alpha)

    cap = CAP_FACTOR * S // E
    idx, pos, off = _topk(aff_t, cap)
    h1_3 = h1.reshape(B, S, D)
    ye = _expert_ffn(idx, h1_3, w_gate, w_up, w_down)
    TB = 256
    starts = jnp.concatenate([off[:, :, ::TB // 128, 0], jnp.full((B, E, 1), cap, I32)], axis=2)
    return _combine(starts, h1_3, pos.reshape(B, E, S), aff_t, ye, l2g, l2b, alpha, TB=TB).reshape(T, D)


def kernel(x, ln_in_g, ln_in_b, hgrn_lb_logits, w_in, b_in, conv_w, mlstm_norm_g, hgrn_norm_g,
           w_branch_m, w_branch_h, w_out, ln1_g, ln1_b, w_router, w_gate_e, w_up_e, w_down_e,
           ln2_g, ln2_b):
    B, S, D = x.shape
    depth = w_in.shape[0]
    alpha = (2.0 * depth) ** 0.25
    lb_logits = jnp.swapaxes(hgrn_lb_logits.astype(F32), 0, 1)
    assert depth == 1, "the layer-to-layer hand-over is written for a single layer"
    out = _layer(x.reshape(B * S, D), ln_in_g.reshape(1, D), ln_in_b.reshape(1, D), B, S, lb_logits, 0,
                 w_in[0], b_in[0], conv_w[0], mlstm_norm_g[0], hgrn_norm_g[0], w_branch_m[0],
                 w_branch_h[0], w_out[0], ln1_g[0], ln1_b[0], w_router[0], w_gate_e[0], w_up_e[0],
                 w_down_e[0], ln2_g[0], ln2_b[0], alpha)
    return out.reshape(B, S, D)
```

```python
import functools

import numpy as np
import jax
import jax.numpy as jnp
from jax import lax
from jax.experimental import pallas as pl
from jax.experimental.pallas import tpu as pltpu

F32 = jnp.float32
BF16 = jnp.bfloat16
I32 = jnp.int32

LN_EPS = 1e-5
CAP_FACTOR = 2
MLSTM_CHUNK = 256
HGRN_CHUNK = 64
VMEM_LIMIT = 56 * 1024 * 1024

_NT = (((1,), (1,)), ((), ()))
_TN = (((0,), (0,)), ((), ()))


def _cparams(sem):
    return pltpu.CompilerParams(dimension_semantics=sem, vmem_limit_bytes=VMEM_LIMIT)


def _dot(a, b):
    return jnp.dot(a, b, preferred_element_type=F32)


def _dot_nt(a, b):
    return lax.dot_general(a, b, _NT, preferred_element_type=F32)


def _dot_tn(a, b):
    return lax.dot_general(a, b, _TN, preferred_element_type=F32)


def _dot_exact(a, b):
    return jnp.dot(a, b, preferred_element_type=F32, precision=lax.Precision.HIGHEST)


def _sigmoid(x):
    return 1.0 / (1.0 + jnp.exp(-x))


def _gate_sigmoid(x):
    return 0.5 * jnp.tanh(0.5 * x) + 0.5


def _layer_norm(x, g, b):
    mu = jnp.mean(x, axis=-1, keepdims=True)
    xc = x - mu
    var = jnp.mean(xc * xc, axis=-1, keepdims=True)
    return xc * lax.rsqrt(var + LN_EPS) * g + b


def _ln_in_body(x_ref, g_ref, b_ref, hb_ref):
    hb_ref[...] = _layer_norm(x_ref[...], g_ref[...], b_ref[...]).astype(BF16)


def _ln_in(x2, g, b, tm=512):
    T, D = x2.shape
    return pl.pallas_call(
        _ln_in_body,
        grid=(T // tm,),
        in_specs=[pl.BlockSpec((tm, D), lambda i: (i, 0)),
                  pl.BlockSpec((1, D), lambda i: (0, 0)),
                  pl.BlockSpec((1, D), lambda i: (0, 0))],
        out_specs=pl.BlockSpec((tm, D), lambda i: (i, 0)),
        out_shape=jax.ShapeDtypeStruct((T, D), BF16),
        compiler_params=_cparams(("arbitrary",)),
        name="ln_in",
    )(x2, g, b)


def _proj_body(x_ref, w_ref, b_ref, o_ref):
    o_ref[...] = (_dot(x_ref[...], w_ref[...]) + b_ref[...]).astype(o_ref.dtype)


def _proj(xb, w, b, out_dtype, tm, tn):
    T, K = xb.shape
    N = w.shape[1]
    return pl.pallas_call(
        _proj_body,
        grid=(N // tn, T // tm),
        in_specs=[pl.BlockSpec((tm, K), lambda j, i: (i, 0)),
                  pl.BlockSpec((K, tn), lambda j, i: (0, j)),
                  pl.BlockSpec((1, tn), lambda j, i: (0, j))],
        out_specs=pl.BlockSpec((tm, tn), lambda j, i: (i, j)),
        out_shape=jax.ShapeDtypeStruct((T, N), out_dtype),
        compiler_params=_cparams(("arbitrary", "arbitrary")),
        name="in_proj",
    )(xb, w, b.reshape(1, N))


def _conv_body(prev_ref, cur_ref, nxt_ref, w_ref, s_ref, o_ref, *, tr, ksz):
    i = pl.program_id(1)
    n = pl.num_programs(1)
    keep_prev = jnp.where(i > 0, 1.0, 0.0).astype(F32)
    keep_next = jnp.where(i < n - 1, 1.0, 0.0).astype(F32)
    xp = jnp.concatenate([prev_ref[0].astype(F32) * keep_prev,
                          cur_ref[0].astype(F32),
                          nxt_ref[0].astype(F32) * keep_next], axis=0)
    rows = tr + 16
    pad = ksz // 2
    acc = jnp.zeros((tr, xp.shape[1]), F32)
    for j in range(ksz):
        d = j - pad
        sh = xp if d == 0 else pltpu.roll(xp, (-d) % rows, 0)
        acc = acc + w_ref[j:j + 1, :] * sh[8:8 + tr, :]
    y = acc * _gate_sigmoid(acc)
    o_ref[0] = (y * s_ref[...]).astype(o_ref.dtype)


def _qk_conv(proj3, conv_w, scale, C, tr=512):
    B, S, _ = proj3.shape
    ksz = conv_w.shape[0]
    hb = tr // 8
    nhb = S // 8
    return pl.pallas_call(
        functools.partial(_conv_body, tr=tr, ksz=ksz),
        grid=(B, S // tr),
        in_specs=[pl.BlockSpec((1, 8, C), lambda b, i: (b, jnp.maximum(i * hb - 1, 0), 0)),
                  pl.BlockSpec((1, tr, C), lambda b, i: (b, i, 0)),
                  pl.BlockSpec((1, 8, C), lambda b, i: (b, jnp.minimum((i + 1) * hb, nhb - 1), 0)),
                  pl.BlockSpec((ksz, C), lambda b, i: (0, 0)),
                  pl.BlockSpec((1, C), lambda b, i: (0, 0))],
        out_specs=pl.BlockSpec((1, tr, C), lambda b, i: (b, i, 0)),
        out_shape=jax.ShapeDtypeStruct((B, S, C), BF16),
        compiler_params=_cparams(("arbitrary", "arbitrary")),
        name="qk_conv",
    )(proj3, proj3, proj3, conv_w, scale)


def _log_sigmoid(x):
    return jnp.minimum(x, 0.0) - jnp.log1p(jnp.exp(-jnp.abs(x)))


def _mlstm_stage1(g_ref, gt_ref, e_ref, rows_ref, slot, *, d, L, nh, dk):
    row = lax.broadcasted_iota(I32, (L, L), 0)
    col = lax.broadcasted_iota(I32, (L, L), 1)
    if d == 0:
        causal = col <= row
        causal_t = col >= row
        last = L - 1
    else:
        causal = col >= row
        causal_t = col <= row
        last = 0
    gcols = g_ref[0]
    grows = gt_ref[0]
    cs_col = _dot_exact(jnp.where(causal, 1.0, 0.0).astype(F32), _log_sigmoid(gcols))
    cs_row = _dot_exact(_log_sigmoid(grows), jnp.where(causal_t, 1.0, 0.0).astype(F32))
    for h in range(nh):
        ci = d * nh + h
        cf = 2 * nh + d * nh + h
        sidx = d * nh + h
        gc = cs_col[:, cf:cf + 1]
        dm = jnp.where(causal, gc - cs_row[cf:cf + 1, :] + grows[ci:ci + 1, :], -jnp.inf)
        rm = jnp.max(dm, axis=1, keepdims=True)
        e_ref[slot, sidx] = jnp.exp(dm - rm)
        rows_ref[slot, sidx, 0] = jnp.broadcast_to(rm, (L, dk))
        rows_ref[slot, sidx, 1] = jnp.broadcast_to(gc, (L, dk))
        rows_ref[slot, sidx, 2] = jnp.broadcast_to(gc[last:last + 1, :] - gc + gcols[:, ci:ci + 1], (L, dk))


def _mlstm_stage2(qk_ref, v_ref, o_ref, c_ref, n_ref, m_ref, e_ref, rows_ref, slot, *, d, L, nh, dk, dv):
    last = L - 1 if d == 0 else 0
    rep = dv // dk
    wide = lambda x: jnp.concatenate([x] * rep, axis=1)
    ones = jnp.ones((L, dk), BF16)
    qk = qk_ref[0]
    v_all = v_ref[0]
    for h in range(nh):
        sidx = d * nh + h
        rm = rows_ref[slot, sidx, 0]
        gc = rows_ref[slot, sidx, 1]
        mprev = m_ref[sidx][0:1, :]
        inter = gc + mprev
        mt = jnp.maximum(rm, inter)
        corr = jnp.exp(rm - mt)
        sc = jnp.exp(inter - mt)
        q = qk[:, h * dk:(h + 1) * dk]
        k = qk[:, (nh + h) * dk:(nh + h + 1) * dk]
        v = v_all[:, h * dv:(h + 1) * dv]
        p = (e_ref[slot, sidx] * _dot_nt(q, k)).astype(BF16)
        cst = c_ref[sidx]
        nst = n_ref[sidx]
        num = wide(corr) * _dot(p, v) + wide(sc) * _dot(q, cst.astype(BF16))
        qn = _dot_nt(q, jnp.concatenate([nst] * (dk // 8), axis=0).astype(BF16))
        den = corr * _dot(p, ones) + sc * qn
        o_ref[0, :, h * dv:(h + 1) * dv] = num / wide(jnp.maximum(jnp.abs(den), jnp.exp(-mt)))
        mnew = mt[last:last + 1, :]
        wk = jnp.exp(rows_ref[slot, sidx, 2] - mnew)
        decay = jnp.exp(gc[last:last + 1, :] + mprev - mnew)
        kw = k.astype(F32) * wk
        c_ref[sidx] = wide(decay) * cst + _dot_tn(kw.astype(BF16), v)
        n_ref[sidx] = jnp.broadcast_to(decay * nst[0:1, :] + jnp.sum(kw, axis=0, keepdims=True), (8, dk))
        m_ref[sidx] = jnp.broadcast_to(mnew, (8, dk))


def _mlstm_body(g_f, gt_f, g_b, gt_b, qk_f, v_f, qk_b, v_b, of_ref, ob_ref, c_ref, n_ref, m_ref,
                e_ref, rows_ref, *, L, nh, dk, dv):
    j = pl.program_id(1)

    @pl.when(j == 0)
    def _():
        e_ref[1] = jnp.zeros(e_ref.shape[1:], F32)
        rows_ref[1] = jnp.zeros(rows_ref.shape[1:], F32)

    @pl.when(j <= 1)
    def _():
        c_ref[...] = jnp.zeros_like(c_ref)
        n_ref[...] = jnp.zeros_like(n_ref)
        m_ref[...] = jnp.zeros_like(m_ref)

    def step(cur):
        for d, (g, gt) in enumerate(((g_f, gt_f), (g_b, gt_b))):
            _mlstm_stage1(g, gt, e_ref, rows_ref, cur, d=d, L=L, nh=nh, dk=dk)
        for d, (qk, v, o) in enumerate(((qk_f, v_f, of_ref), (qk_b, v_b, ob_ref))):
            _mlstm_stage2(qk, v, o, c_ref, n_ref, m_ref, e_ref, rows_ref, 1 - cur,
                          d=d, L=L, nh=nh, dk=dk, dv=dv)

    for parity in (0, 1):
        pl.when(j % 2 == parity)(functools.partial(step, parity))


def _mlstm(qk, proj3, v_blk, gates, gates_t, nh, dk, dv):
    B, S, _ = qk.shape
    L = MLSTM_CHUNK
    nc = S // L
    G = gates.shape[-1]
    assert dv % dk == 0 and dk % 8 == 0
    prep_f = lambda b, j: (b, jnp.minimum(j, nc - 1))
    prep_b = lambda b, j: (b, jnp.maximum(nc - 1 - j, 0))
    run_f = lambda b, j: (b, jnp.maximum(j - 1, 0))
    run_b = lambda b, j: (b, jnp.minimum(nc - j, nc - 1))
    specs = []
    for im in (prep_f, prep_b):
        specs += [pl.BlockSpec((1, L, G), (lambda b, j, im=im: im(b, j) + (0,))),
                  pl.BlockSpec((1, G, L), (lambda b, j, im=im: (im(b, j)[0], 0, im(b, j)[1])))]
    for im in (run_f, run_b):
        specs += [pl.BlockSpec((1, L, 2 * nh * dk), (lambda b, j, im=im: im(b, j) + (0,))),
                  pl.BlockSpec((1, L, nh * dv), (lambda b, j, im=im: im(b, j) + (v_blk,)))]
    return pl.pallas_call(
        functools.partial(_mlstm_body, L=L, nh=nh, dk=dk, dv=dv),
        grid=(B, nc + 1),
        in_specs=specs,
        out_specs=[pl.BlockSpec((1, L, nh * dv), lambda b, j: run_f(b, j) + (0,)),
                   pl.BlockSpec((1, L, nh * dv), lambda b, j: run_b(b, j) + (0,))],
        out_shape=[jax.ShapeDtypeStruct((B, S, nh * dv), F32)] * 2,
        scratch_shapes=[pltpu.VMEM((2 * nh, dk, dv), F32),
                        pltpu.VMEM((2 * nh, 8, dk), F32),
                        pltpu.VMEM((2 * nh, 8, dk), F32),
                        pltpu.VMEM((2, 2 * nh, L, L), F32),
                        pltpu.VMEM((2, 2 * nh, 3, L, dk), F32)],
        compiler_params=_cparams(("arbitrary", "arbitrary")),
        name="mlstm_scan",
    )(gates, gates_t, gates, gates_t, qk, proj3, qk, proj3)


def _hgrn_levels(L):
    c = L // 2
    out = []
    while c >= 1:
        out.append(c)
        c //= 2
    return out


def _hgrn_sum_matrix(L, d):
    t = np.arange(L)[:, None]
    r = np.arange(L)[None, :]
    if d == 0:
        blocks = [(r <= t), (r > t)]
    else:
        blocks = [(r >= t), (r < t)]
    for c in _hgrn_levels(L):
        base = (t // (2 * c)) * (2 * c)
        mid = base + c
        second = (t - base) >= c
        if d == 0:
            m = np.where(second, (r >= mid) & (r <= t), (r > t) & (r < mid))
        else:
            m = np.where(second, (r >= mid) & (r < t), (r >= t) & (r < mid))
        blocks.append(m)
    p = np.concatenate(blocks, axis=0).astype(np.float32)
    return np.concatenate([p, p, p], axis=1)


def _hgrn_prep(f_ref, lb_ref, p_ref, kb_ref, sums_ref, *, d, layer):
    slots = [lb_ref[k][d:d + 1, :] for k in range(lb_ref.shape[0])]
    top = functools.reduce(jnp.maximum, slots)
    es = [jnp.exp(s - top) for s in slots]
    lb = sum(es[:layer + 1]) / sum(es)
    f = lb + (1.0 - lb) * _sigmoid(f_ref[0])
    lgf = jnp.log2(f)
    kb_ref[d] = 1.0 - f
    hi = lgf.astype(BF16)
    r1 = lgf - hi.astype(F32)
    mid = r1.astype(BF16)
    lo = (r1 - mid.astype(F32)).astype(BF16)
    sums_ref[d] = _dot(p_ref[...], jnp.concatenate([hi, mid, lo], axis=0))


def _hgrn_head(q_ref, i_ref, o_ref, st_ref, kb_ref, sums_ref, h, *, d, L, nh, dk, dv):
    levels = _hgrn_levels(L)
    rowi = lax.broadcasted_iota(I32, (L, 1), 0)
    xor = lax.broadcasted_iota(I32, (L, L), 0) ^ lax.broadcasted_iota(I32, (L, L), 1)
    last = L - 1 if d == 0 else 0
    ks = slice(h * dk, (h + 1) * dk)
    vs = slice(h * dv, (h + 1) * dv)
    qh = q_ref[0, :, ks].astype(F32)
    kbh = kb_ref[d, :, ks]
    ih = i_ref[0, :, vs]
    eb = jnp.exp2(sums_ref[d, 0:L, ks])
    ea = jnp.exp2(sums_ref[d, L:2 * L, ks])
    a = jnp.zeros((L, L), F32)
    for li, c in enumerate(levels):
        k = c.bit_length() - 1
        bit = (rowi >> k) & 1
        if d == 1:
            bit = 1 - bit
        el = jnp.exp2(sums_ref[d, (2 + li) * L:(3 + li) * L, ks])
        z = (jnp.where(bit == 1, qh, kbh) * el).astype(BF16)
        pair = ((xor >> k) + ((1 - bit) << 8)) == 1
        a = jnp.where(pair, _dot_nt(z, z), a)
    st = st_ref[d * nh + h]
    o = (_dot(a.astype(BF16), ih) + _dot_nt((qh * eb).astype(BF16), st.astype(BF16))
         + jnp.sum(qh * kbh, axis=1, keepdims=True) * ih.astype(F32))
    o_ref[0, :, vs] = o
    st_ref[d * nh + h] = st * eb[last:last + 1, :] + _dot_tn(ih, (kbh * ea).astype(BF16))


def _hgrn_body(q_f, i_f, f_f, q_b, i_b, f_b, lb_ref, pf_ref, pb_ref, of_ref, ob_ref, st_ref,
               kb_ref, sums_ref, *, L, nh, dk, dv, layer):
    @pl.when(pl.program_id(1) == 0)
    def _():
        st_ref[...] = jnp.zeros_like(st_ref)

    _hgrn_prep(f_f, lb_ref, pf_ref, kb_ref, sums_ref, d=0, layer=layer)
    _hgrn_prep(f_b, lb_ref, pb_ref, kb_ref, sums_ref, d=1, layer=layer)
    kw = dict(L=L, nh=nh, dk=dk, dv=dv)

    for h in range(nh):
        _hgrn_head(q_f, i_f, of_ref, st_ref, kb_ref, sums_ref, h, d=0, **kw)
        _hgrn_head(q_b, i_b, ob_ref, st_ref, kb_ref, sums_ref, h, d=1, **kw)


def _hgrn(proj3, q_blk, i_blk, fpre, lb, layer, nh, dk, dv):
    B, S, _ = proj3.shape
    L = HGRN_CHUNK
    nc = S // L
    W = nh * dk
    pf = jnp.asarray(_hgrn_sum_matrix(L, 0), BF16)
    pb = jnp.asarray(_hgrn_sum_matrix(L, 1), BF16)
    fw = lambda b, j: (b, j)
    bw = lambda b, j: (b, nc - 1 - j)
    specs = []
    for d, im in enumerate((fw, bw)):
        specs += [pl.BlockSpec((1, L, W), (lambda b, j, im=im: im(b, j) + (q_blk,))),
                  pl.BlockSpec((1, L, nh * dv), (lambda b, j, im=im: im(b, j) + (i_blk,))),
                  pl.BlockSpec((1, L, W), (lambda b, j, im=im, d=d: im(b, j) + (d,)))]
    specs += [pl.BlockSpec(lb.shape, lambda b, j: (0, 0, 0)),
              pl.BlockSpec(pf.shape, lambda b, j: (0, 0)),
              pl.BlockSpec(pb.shape, lambda b, j: (0, 0))]
    return pl.pallas_call(
        functools.partial(_hgrn_body, L=L, nh=nh, dk=dk, dv=dv, layer=layer),
        grid=(B, nc),
        in_specs=specs,
        out_specs=[pl.BlockSpec((1, L, nh * dv), lambda b, j: (b, j, 0)),
                   pl.BlockSpec((1, L, nh * dv), lambda b, j: (b, nc - 1 - j, 0))],
        out_shape=[jax.ShapeDtypeStruct((B, S, nh * dv), F32)] * 2,
        scratch_shapes=[pltpu.VMEM((2 * nh, dv, dk), F32),
                        pltpu.VMEM((2, L, W), F32),
                        pltpu.VMEM((2, pf.shape[0], W), F32)],
        compiler_params=_cparams(("arbitrary", "arbitrary")),
        name="hgrn2_scan",
    )(proj3, proj3, fpre, proj3, proj3, fpre, lb, pf, pb)


def _head_rms(x, nh, dh):
    outs = []
    for h in range(nh):
        xh = x[:, h * dh:(h + 1) * dh]
        outs.append(xh * lax.rsqrt(jnp.mean(xh * xh, axis=-1, keepdims=True) + LN_EPS))
    return jnp.concatenate(outs, axis=1)


def _post_body(hmf, hmb, hof, hob, om, gh, gm, ghh, x_ref, l0g, l0b, mg, hg, wbm, wbh, wout, l1g, l1b, wr,
               h1_ref, aff_ref, *, nh_m, dv_m, nh_h, dv_h, alpha):
    hm = _head_rms(hmf[...] + hmb[...], nh_m, dv_m) * mg[...] * _gate_sigmoid(om[...].astype(F32))
    hgate = gh[...].astype(F32)
    ho = _head_rms(hof[...] + hob[...], nh_h, dv_h) * hg[...] * (hgate * _gate_sigmoid(hgate))
    y_m = _dot(hm.astype(BF16), wbm[...])
    y_h = _dot(ho.astype(BF16), wbh[...])
    merged = _gate_sigmoid(gm[...].astype(F32)) * y_m + _gate_sigmoid(ghh[...].astype(F32)) * y_h
    mix = _dot(merged.astype(BF16), wout[...])
    h = _layer_norm(x_ref[...], l0g[...], l0b[...])
    h1 = _layer_norm(alpha * h + mix, l1g[...], l1b[...])
    h1_ref[...] = h1
    logits = _dot_nt(wr[...], h1.astype(BF16))
    z = jnp.exp(logits - jnp.max(logits, axis=0, keepdims=True))
    aff_ref[0] = z / jnp.sum(z, axis=0, keepdims=True)


def _post_mixer(hm_f, hm_b, ho_f, ho_b, proj, cols, x2, l0g, l0b, mg, hg, wbm, wbh, wout, l1g, l1b, wr_t,
                B, S, dims, alpha, tm=256):
    T, D = x2.shape
    nh_m, dv_m, nh_h, dv_h = dims
    vm, vh = nh_m * dv_m, nh_h * dv_h
    E = wr_t.shape[0]
    nbs = S // tm
    row = lambda i: (i, 0)
    const = lambda i: (0, 0)
    in_specs = [pl.BlockSpec((tm, vm), row), pl.BlockSpec((tm, vm), row),
                pl.BlockSpec((tm, vh), row), pl.BlockSpec((tm, vh), row),
                pl.BlockSpec((tm, vm), lambda i: (i, cols["o_m"] // vm)),
                pl.BlockSpec((tm, vh), lambda i: (i, cols["g_h"] // vh)),
                pl.BlockSpec((tm, D), lambda i: (i, cols["gate_m"] // D)),
                pl.BlockSpec((tm, D), lambda i: (i, cols["gate_h"] // D)),
                pl.BlockSpec((tm, D), row), pl.BlockSpec((1, D), const), pl.BlockSpec((1, D), const),
                pl.BlockSpec((1, vm), const), pl.BlockSpec((1, vh), const),
                pl.BlockSpec((vm, D), const), pl.BlockSpec((vh, D), const), pl.BlockSpec((D, D), const),
                pl.BlockSpec((1, D), const), pl.BlockSpec((1, D), const),
                pl.BlockSpec((E, D), const)]
    return pl.pallas_call(
        functools.partial(_post_body, nh_m=nh_m, dv_m=dv_m, nh_h=nh_h, dv_h=dv_h, alpha=alpha),
        grid=(T // tm,),
        in_specs=in_specs,
        out_specs=[pl.BlockSpec((tm, D), row),
                   pl.BlockSpec((1, E, tm), lambda i: (i // nbs, 0, i % nbs))],
        out_shape=[jax.ShapeDtypeStruct((T, D), F32), jax.ShapeDtypeStruct((B, E, S), F32)],
        compiler_params=_cparams(("arbitrary",)),
        name="mixer_out_ln1_router",
    )(hm_f, hm_b, ho_f, ho_b, proj, proj, proj, proj, x2, l0g, l0b, mg, hg, wbm, wbh, wout, l1g, l1b, wr_t)


def _prefix_counts(mask2, upper, ones, bdl):
    mb = mask2.astype(BF16)
    within = _dot(mb, upper)
    rowtot = _dot(mb, ones)
    before = _dot(bdl, rowtot.astype(BF16))
    return within + before, within


def _topk_body(a_ref, upper_ref, ones_ref, bdl_ref, idx_ref, pos_ref, off_ref, *, E, R, cap):
    a3 = a_ref[0]

    def count(m):
        return jnp.sum(jnp.sum(jnp.where(m, 1.0, 0.0), axis=2, keepdims=True), axis=1, keepdims=True)

    def as_float(bits):
        return pltpu.bitcast(jnp.broadcast_to(bits, a3.shape), F32)

    def step(it, cur):
        cand = cur | (jnp.int32(1) << (30 - it))
        return jnp.where(count(a3 >= as_float(cand)) >= cap, cand, cur)

    thr = lax.fori_loop(0, 31, step, jnp.zeros((E, 1, 1), I32))
    gt = a3 >= as_float(thr + 1)
    eq = jnp.logical_and(a3 >= as_float(thr), jnp.logical_not(gt))
    need = cap - count(gt)
    upper, ones, bdl = upper_ref[...], ones_ref[...], bdl_ref[...]
    eq_incl, _ = _prefix_counts(jnp.where(eq, 1.0, 0.0).reshape(E * R, 128), upper, ones, bdl)
    sel = jnp.logical_or(gt, jnp.logical_and(eq, eq_incl.reshape(E, R, 128) <= need))
    self32 = jnp.where(sel, 1.0, 0.0).reshape(E * R, 128)
    incl, within = _prefix_counts(self32, upper, ones, bdl)
    pos_ref[0] = jnp.where(sel, incl.reshape(E, R, 128) - 1.0, -1.0).astype(I32)

    slot = lax.broadcasted_iota(I32, (1, cap), 1).astype(F32)
    rsub = lax.broadcasted_iota(I32, (R, cap), 0).astype(F32)
    for e in range(E):
        inc_e = incl[e * R:(e + 1) * R]
        row_incl = inc_e[:, 127:128]
        row_excl = row_incl - within[e * R:(e + 1) * R][:, 127:128]
        off_ref[0, e] = jnp.broadcast_to(row_excl, (R, 128)).astype(I32)
        ridx = jnp.sum(jnp.where(row_incl <= slot, 1.0, 0.0), axis=0, keepdims=True)
        onehot = rsub == ridx
        start = jnp.sum(jnp.where(onehot, row_excl, 0.0), axis=0, keepdims=True)
        local = slot - start
        pg = _dot_tn(within[e * R:(e + 1) * R].astype(BF16), jnp.where(onehot, 1.0, 0.0).astype(BF16))
        lane = jnp.sum(jnp.where(pg <= local, 1.0, 0.0), axis=0, keepdims=True)
        idx_ref[0, e:e + 1, :] = (ridx * 128.0 + lane).astype(I32)


def _topk(aff_t, cap):
    B, E, S = aff_t.shape
    R = S // 128
    k = np.arange(128)
    upper = jnp.asarray((k[:, None] <= k[None, :]).astype(np.float32), BF16)
    ones = jnp.ones((128, 128), BF16)
    r = np.arange(E * R)
    bdl = jnp.asarray(((r[:, None] // R == r[None, :] // R) & (r[None, :] < r[:, None])).astype(np.float32), BF16)
    const2 = lambda b: (0, 0)
    return pl.pallas_call(
        functools.partial(_topk_body, E=E, R=R, cap=cap),
        grid=(B,),
        in_specs=[pl.BlockSpec((1, E, R, 128), lambda b: (b, 0, 0, 0)),
                  pl.BlockSpec((128, 128), const2), pl.BlockSpec((128, 128), const2),
                  pl.BlockSpec((E * R, E * R), const2)],
        out_specs=[pl.BlockSpec((1, E, cap), lambda b: (b, 0, 0)),
                   pl.BlockSpec((1, E, R, 128), lambda b: (b, 0, 0, 0)),
                   pl.BlockSpec((1, E, R, 128), lambda b: (b, 0, 0, 0))],
        out_shape=[jax.ShapeDtypeStruct((B, E, cap), I32),
                   jax.ShapeDtypeStruct((B, E, R, 128), I32),
                   jax.ShapeDtypeStruct((B, E, R, 128), I32)],
        compiler_params=_cparams(("arbitrary",)),
        name="expert_topk",
    )(aff_t.reshape(B, E, R, 128), upper, ones, bdl)


def _ffn_body(idx_ref, h_hbm, wg_ref, wu_ref, wd_ref, ye_ref, xg, xb, acc, sem, *, E, B, C, rps):
    e = pl.program_id(0)
    b = pl.program_id(1)
    f = pl.program_id(2)
    nf = pl.num_programs(2)
    nrows = xg.shape[1]
    pair = e * B + b
    slot = pair % 2
    nxt = (pair + 1) % (E * B)
    e_n = nxt // B
    b_n = nxt % B

    def row_copy(bb, src_row, sl, dst_row):
        return pltpu.make_async_copy(h_hbm.at[bb, pl.ds(src_row, 1), :],
                                     xg.at[sl, pl.ds(dst_row, 1), :], sem.at[sl])

    def src_row(bb, ee, s):
        return idx_ref[(bb * E + ee) * C + jnp.minimum(s, C - 1)]

    def wait_all(sl):
        def wait(s, carry):
            row_copy(0, 0, sl, s).wait()
            return carry
        lax.fori_loop(0, nrows, wait, 0, unroll=rps)

    @pl.when(f == 0)
    def _():
        @pl.when(pair == 0)
        def _():
            def issue(s, carry):
                row_copy(b, src_row(b, e, s), slot, s).start()
                return carry
            lax.fori_loop(0, nrows, issue, 0)

        wait_all(slot)
        xb[...] = xg[slot, 0:C, :].astype(BF16)
        acc[...] = jnp.zeros_like(acc)

    for r in range(rps):
        s = f * rps + r
        row_copy(b_n, src_row(b_n, e_n, s), 1 - slot, s).start()

    x = xb[...]
    g = _dot(x, wg_ref[0].astype(BF16))
    u = _dot(x, wu_ref[0].astype(BF16))
    hid = (g * _gate_sigmoid(g) * u).astype(BF16)
    acc[...] += _dot(hid, wd_ref[0].astype(BF16))

    @pl.when(f == nf - 1)
    def _():
        ye_ref[0, 0] = acc[...].astype(ye_ref.dtype)

        @pl.when(pair == E * B - 1)
        def _():
            wait_all(1 - slot)


def _expert_ffn(idx, h1, w_gate, w_up, w_down, tf=256):
    B, S, D = h1.shape
    E, _, F = w_gate.shape
    C = idx.shape[-1]
    nf = F // tf
    rps = pl.cdiv(C, nf)
    grid_spec = pltpu.PrefetchScalarGridSpec(
        num_scalar_prefetch=1,
        grid=(E, B, nf),
        in_specs=[pl.BlockSpec(memory_space=pl.ANY),
                  pl.BlockSpec((1, D, tf), lambda e, b, f, idx: (e, 0, f)),
                  pl.BlockSpec((1, D, tf), lambda e, b, f, idx: (e, 0, f)),
                  pl.BlockSpec((1, tf, D), lambda e, b, f, idx: (e, f, 0))],
        out_specs=pl.BlockSpec((1, 1, C, D), lambda e, b, f, idx: (b, e, 0, 0)),
        scratch_shapes=[pltpu.VMEM((2, nf * rps, D), F32), pltpu.VMEM((C, D), BF16),
                        pltpu.VMEM((C, D), F32), pltpu.SemaphoreType.DMA((2,))],
    )
    return pl.pallas_call(
        functools.partial(_ffn_body, E=E, B=B, C=C, rps=rps),
        grid_spec=grid_spec,
        out_shape=jax.ShapeDtypeStruct((B, E, C, D), BF16),
        compiler_params=_cparams(("arbitrary", "arbitrary", "arbitrary")),
        name="expert_ffn",
    )(idx.reshape(-1), h1, w_gate, w_up, w_down)


COMBINE_ROWS = 16
COMBINE_KT = 256


def _combine_body(st_ref, h_ref, pos_ref, aff_ref, ye_hbm, g_ref, b_ref, o_ref, buf, acc, sem,
                  *, E, B, TB, NTB, alpha):
    b = pl.program_id(0)
    tb = pl.program_id(1)
    rows, kt = COMBINE_ROWS, COMBINE_KT
    step = b * NTB + tb
    slot = step % 2

    def chunk_copy(bb, e, src, sl, dst):
        return pltpu.make_async_copy(ye_hbm.at[bb, e, pl.ds(src, rows), :],
                                     buf.at[sl, pl.ds(dst, rows), :], sem.at[sl])

    def plan(bb, tt):
        koff = jnp.int32(0)
        out = []
        for e in range(E):
            base = (bb * E + e) * (NTB + 1) + tt
            ws = (st_ref[base] // rows) * rows
            n = (st_ref[base + 1] - ws + rows - 1) // rows
            out.append((ws, n, koff))
            koff = koff + n * rows
        return out, koff

    def issue_all(bb, tt, sl):
        for e, (ws, n, koff) in enumerate(plan(bb, tt)[0]):
            def issue(c, carry, e=e, ws=ws, koff=koff):
                chunk_copy(bb, e, pl.multiple_of(ws + c * rows, rows), sl,
                           pl.multiple_of(koff + c * rows, rows)).start()
                return carry
            lax.fori_loop(0, n, issue, 0)

    @pl.when(step == 0)
    def _():
        buf[...] = jnp.zeros_like(buf)
        issue_all(b, tb, slot)

    cur, ktot = plan(b, tb)

    def wait(c, carry):
        chunk_copy(0, 0, 0, slot, 0).wait()
        return carry

    lax.fori_loop(0, ktot // rows, wait, 0)

    @pl.when(step + 1 < B * NTB)
    def _():
        issue_all((step + 1) // NTB, (step + 1) % NTB, 1 - slot)

    pos = pos_ref[0]
    aff = aff_ref[0]
    keys = []
    for e, (ws, n, koff) in enumerate(cur):
        pe = pos[e:e + 1, :]
        keys.append(jnp.where(pe >= 0, pe + (koff - ws), -1))

    def onehot_t(kbase):
        ksub = lax.broadcasted_iota(I32, (kt, TB), 0) + kbase
        o = jnp.zeros((kt, TB), F32)
        for e in range(E):
            o = jnp.where(ksub == keys[e], aff[e:e + 1, :], o)
        return o.astype(BF16)

    acc[...] = _dot_tn(onehot_t(0), buf[slot, 0:kt, :])

    def kstep(kc, carry):
        kbase = pl.multiple_of(kc * kt, kt)
        acc[...] += _dot_tn(onehot_t(kbase), buf[slot, pl.ds(kbase, kt), :])
        return carry

    lax.fori_loop(1, (ktot + kt - 1) // kt, kstep, 0)
    o_ref[0] = _layer_norm(alpha * h_ref[0] + acc[...], g_ref[...], b_ref[...])


def _combine(starts, h1, pos, aff, ye, g, bb, alpha, TB):
    B, S, D = h1.shape
    E = ye.shape[1]
    NTB = S // TB
    kmax = pl.cdiv(E * (TB + 2 * COMBINE_ROWS), COMBINE_KT) * COMBINE_KT
    grid_spec = pltpu.PrefetchScalarGridSpec(
        num_scalar_prefetch=1,
        grid=(B, NTB),
        in_specs=[pl.BlockSpec((1, TB, D), lambda b, t, st: (b, t, 0)),
                  pl.BlockSpec((1, E, TB), lambda b, t, st: (b, 0, t)),
                  pl.BlockSpec((1, E, TB), lambda b, t, st: (b, 0, t)),
                  pl.BlockSpec(memory_space=pl.ANY),
                  pl.BlockSpec((1, D), lambda b, t, st: (0, 0)),
                  pl.BlockSpec((1, D), lambda b, t, st: (0, 0))],
        out_specs=pl.BlockSpec((1, TB, D), lambda b, t, st: (b, t, 0)),
        scratch_shapes=[pltpu.VMEM((2, kmax, D), BF16), pltpu.VMEM((TB, D), F32),
                        pltpu.SemaphoreType.DMA((2,))],
    )
    return pl.pallas_call(
        functools.partial(_combine_body, E=E, B=B, TB=TB, NTB=NTB, alpha=alpha),
        grid_spec=grid_spec,
        out_shape=jax.ShapeDtypeStruct((B, S, D), F32),
        compiler_params=_cparams(("arbitrary", "arbitrary")),
        name="combine_ln2",
    )(starts.reshape(-1), h1, pos, aff, ye, g.reshape(1, D), bb.reshape(1, D))


def _layer(x2, l0g, l0b, B, S, lb, layer, w_in, b_in, conv_w, mg, hg, w_bm, w_bh, w_out, l1g, l1b,
           w_router, w_gate, w_up, w_down, l2g, l2b, alpha):
    T, D = x2.shape
    nh_m, dv_m = mg.shape
    nh_h, dv_h = hg.shape
    ksz, qk2 = conv_w.shape
    dk_m = qk2 // (2 * nh_m)
    v_m = nh_m * dv_m
    v_h = nh_h * dv_h
    q_h = lb.shape[-1]
    dk_h = q_h // nh_h
    E = w_router.shape[1]
    ng = 4 * nh_m
    sizes = (qk2, v_m, v_m, ng // 2, ng // 2, q_h, 2 * q_h, v_h, v_h, D, D)
    names = ("qk_m", "v_m", "o_m", "i_m", "f_m", "q_h", "f_h", "i_h", "g_h", "gate_m", "gate_h")
    start = dict(zip(names, np.concatenate([[0], np.cumsum(sizes)[:-1]]).tolist()))
    size = dict(zip(names, sizes))
    sl = lambda a, n: a[..., start[n]:start[n] + size[n]]
    a_names = ("qk_m", "v_m", "o_m", "q_h", "i_h", "g_h", "gate_m", "gate_h")
    w_a = jnp.concatenate([sl(w_in, n) for n in a_names], axis=1).astype(BF16)
    b_a = jnp.concatenate([sl(b_in, n) for n in a_names])
    cols = dict(zip(a_names, np.concatenate([[0], np.cumsum([size[n] for n in a_names])[:-1]]).tolist()))
    gpad = 128 - ng
    g_names = ("f_h", "i_m", "f_m")
    w_fg = jnp.pad(jnp.concatenate([sl(w_in, n) for n in g_names], axis=1), ((0, 0), (0, gpad))).astype(BF16)
    b_fg = jnp.pad(jnp.concatenate([sl(b_in, n) for n in g_names]), (0, gpad))

    hb = _ln_in(x2, l0g, l0b)
    proj = _proj(hb, w_a, b_a, BF16, tm=1024, tn=1024)
    fg = _proj(hb, w_fg, b_fg, F32, tm=1024, tn=w_fg.shape[1])
    gates = fg[:, 2 * q_h:2 * q_h + ng].reshape(B, S, ng)
    proj3 = proj.reshape(B, S, -1)

    kscale = jnp.concatenate([jnp.ones((qk2 // 2,), F32), jnp.full((qk2 // 2,), dk_m ** -0.5, F32)])
    qk = _qk_conv(proj3, conv_w, kscale.reshape(1, qk2), qk2)
    hm_f, hm_b = _mlstm(qk, proj3, cols["v_m"] // v_m, gates, jnp.swapaxes(gates, 1, 2), nh_m, dk_m, dv_m)

    ho_f, ho_b = _hgrn(proj3, cols["q_h"] // q_h, cols["i_h"] // v_h, fg.reshape(B, S, -1), lb,
                       layer, nh_h, dk_h, dv_h)

    h1, aff_t = _post_mixer(hm_f.reshape(T, v_m), hm_b.reshape(T, v_m), ho_f.reshape(T, v_h),
                            ho_b.reshape(T, v_h), proj, cols, x2, l0g, l0b, mg.reshape(1, v_m), hg.reshape(1, v_h),
                            w_bm.astype(BF16), w_bh.astype(BF16), w_out.astype(BF16),
                            l1g.reshape(1, D), l1b.reshape(1, D), w_router.T.astype(BF16),
                            B, S, (nh_m, dv_m, nh_h, dv_h), alpha)

    cap = CAP_FACTOR * S // E
    idx, pos, off = _topk(aff_t, cap)
    h1_3 = h1.reshape(B, S, D)
    ye = _expert_ffn(idx, h1_3, w_gate, w_up, w_down)
    TB = 256
    starts = jnp.concatenate([off[:, :, ::TB // 128, 0], jnp.full((B, E, 1), cap, I32)], axis=2)
    return _combine(starts, h1_3, pos.reshape(B, E, S), aff_t, ye, l2g, l2b, alpha, TB=TB).reshape(T, D)


def kernel(x, ln_in_g, ln_in_b, hgrn_lb_logits, w_in, b_in, conv_w, mlstm_norm_g, hgrn_norm_g,
           w_branch_m, w_branch_h, w_out, ln1_g, ln1_b, w_router, w_gate_e, w_up_e, w_down_e,
           ln2_g, ln2_b):
    B, S, D = x.shape
    depth = w_in.shape[0]
    alpha = (2.0 * depth) ** 0.25
    lb_logits = jnp.swapaxes(hgrn_lb_logits.astype(F32), 0, 1)
    assert depth == 1, "the layer-to-layer hand-over is written for a single layer"
    out = _layer(x.reshape(B * S, D), ln_in_g.reshape(1, D), ln_in_b.reshape(1, D), B, S, lb_logits, 0,
                 w_in[0], b_in[0], conv_w[0], mlstm_norm_g[0], hgrn_norm_g[0], w_branch_m[0],
                 w_branch_h[0], w_out[0], ln1_g[0], ln1_b[0], w_router[0], w_gate_e[0], w_up_e[0],
                 w_down_e[0], ln2_g[0], ln2_b[0], alpha)
    return out.reshape(B, S, D)
```

```python
import functools

import numpy as np
import jax
import jax.numpy as jnp
from jax import lax
from jax.experimental import pallas as pl
from jax.experimental.pallas import tpu as pltpu

F32 = jnp.float32
BF16 = jnp.bfloat16
I32 = jnp.int32

LN_EPS = 1e-5
CAP_FACTOR = 2
MLSTM_CHUNK = 256
HGRN_CHUNK = 64
VMEM_LIMIT = 56 * 1024 * 1024

_NT = (((1,), (1,)), ((), ()))
_TN = (((0,), (0,)), ((), ()))


def _cparams(sem):
    return pltpu.CompilerParams(dimension_semantics=sem, vmem_limit_bytes=VMEM_LIMIT)


def _dot(a, b):
    return jnp.dot(a, b, preferred_element_type=F32)


def _dot_nt(a, b):
    return lax.dot_general(a, b, _NT, preferred_element_type=F32)


def _dot_tn(a, b):
    return lax.dot_general(a, b, _TN, preferred_element_type=F32)


def _dot_exact(a, b):
    return jnp.dot(a, b, preferred_element_type=F32, precision=lax.Precision.HIGHEST)


def _sigmoid(x):
    return 1.0 / (1.0 + jnp.exp(-x))


def _gate_sigmoid(x):
    return 0.5 * jnp.tanh(0.5 * x) + 0.5


def _layer_norm(x, g, b):
    mu = jnp.mean(x, axis=-1, keepdims=True)
    xc = x - mu
    var = jnp.mean(xc * xc, axis=-1, keepdims=True)
    return xc * lax.rsqrt(var + LN_EPS) * g + b


def _ln_in_body(x_ref, g_ref, b_ref, hb_ref):
    hb_ref[...] = _layer_norm(x_ref[...], g_ref[...], b_ref[...]).astype(BF16)


def _ln_in(x2, g, b, tm=512):
    T, D = x2.shape
    return pl.pallas_call(
        _ln_in_body,
        grid=(T // tm,),
        in_specs=[pl.BlockSpec((tm, D), lambda i: (i, 0)),
                  pl.BlockSpec((1, D), lambda i: (0, 0)),
                  pl.BlockSpec((1, D), lambda i: (0, 0))],
        out_specs=pl.BlockSpec((tm, D), lambda i: (i, 0)),
        out_shape=jax.ShapeDtypeStruct((T, D), BF16),
        compiler_params=_cparams(("arbitrary",)),
        name="ln_in",
    )(x2, g, b)


def _proj_body(x_ref, w_ref, b_ref, o_ref):
    o_ref[...] = (_dot(x_ref[...], w_ref[...]) + b_ref[...]).astype(o_ref.dtype)


def _proj(xb, w, b, out_dtype, tm, tn):
    T, K = xb.shape
    N = w.shape[1]
    return pl.pallas_call(
        _proj_body,
        grid=(N // tn, T // tm),
        in_specs=[pl.BlockSpec((tm, K), lambda j, i: (i, 0)),
                  pl.BlockSpec((K, tn), lambda j, i: (0, j)),
                  pl.BlockSpec((1, tn), lambda j, i: (0, j))],
        out_specs=pl.BlockSpec((tm, tn), lambda j, i: (i, j)),
        out_shape=jax.ShapeDtypeStruct((T, N), out_dtype),
        compiler_params=_cparams(("arbitrary", "arbitrary")),
        name="in_proj",
    )(xb, w, b.reshape(1, N))


def _conv_body(prev_ref, cur_ref, nxt_ref, w_ref, s_ref, o_ref, *, tr, ksz):
    i = pl.program_id(1)
    n = pl.num_programs(1)
    keep_prev = jnp.where(i > 0, 1.0, 0.0).astype(F32)
    keep_next = jnp.where(i < n - 1, 1.0, 0.0).astype(F32)
    xp = jnp.concatenate([prev_ref[0].astype(F32) * keep_prev,
                          cur_ref[0].astype(F32),
                          nxt_ref[0].astype(F32) * keep_next], axis=0)
    rows = tr + 16
    pad = ksz // 2
    acc = jnp.zeros((tr, xp.shape[1]), F32)
    for j in range(ksz):
        d = j - pad
        sh = xp if d == 0 else pltpu.roll(xp, (-d) % rows, 0)
        acc = acc + w_ref[j:j + 1, :] * sh[8:8 + tr, :]
    y = acc * _gate_sigmoid(acc)
    o_ref[0] = (y * s_ref[...]).astype(o_ref.dtype)


def _qk_conv(proj3, conv_w, scale, C, tr=512):
    B, S, _ = proj3.shape
    ksz = conv_w.shape[0]
    hb = tr // 8
    nhb = S // 8
    return pl.pallas_call(
        functools.partial(_conv_body, tr=tr, ksz=ksz),
        grid=(B, S // tr),
        in_specs=[pl.BlockSpec((1, 8, C), lambda b, i: (b, jnp.maximum(i * hb - 1, 0), 0)),
                  pl.BlockSpec((1, tr, C), lambda b, i: (b, i, 0)),
                  pl.BlockSpec((1, 8, C), lambda b, i: (b, jnp.minimum((i + 1) * hb, nhb - 1), 0)),
                  pl.BlockSpec((ksz, C), lambda b, i: (0, 0)),
                  pl.BlockSpec((1, C), lambda b, i: (0, 0))],
        out_specs=pl.BlockSpec((1, tr, C), lambda b, i: (b, i, 0)),
        out_shape=jax.ShapeDtypeStruct((B, S, C), BF16),
        compiler_params=_cparams(("arbitrary", "arbitrary")),
        name="qk_conv",
    )(proj3, proj3, proj3, conv_w, scale)


def _log_sigmoid(x):
    return jnp.minimum(x, 0.0) - jnp.log1p(jnp.exp(-jnp.abs(x)))


def _mlstm_stage1(g_ref, gt_ref, e_ref, rows_ref, slot, *, d, L, nh, dk):
    row = lax.broadcasted_iota(I32, (L, L), 0)
    col = lax.broadcasted_iota(I32, (L, L), 1)
    if d == 0:
        causal = col <= row
        causal_t = col >= row
        last = L - 1
    else:
        causal = col >= row
        causal_t = col <= row
        last = 0
    gcols = g_ref[0]
    grows = gt_ref[0]
    cs_col = _dot_exact(jnp.where(causal, 1.0, 0.0).astype(F32), _log_sigmoid(gcols))
    cs_row = _dot_exact(_log_sigmoid(grows), jnp.where(causal_t, 1.0, 0.0).astype(F32))
    for h in range(nh):
        ci = d * nh + h
        cf = 2 * nh + d * nh + h
        sidx = d * nh + h
        gc = cs_col[:, cf:cf + 1]
        dm = jnp.where(causal, gc - cs_row[cf:cf + 1, :] + grows[ci:ci + 1, :], -jnp.inf)
        rm = jnp.max(dm, axis=1, keepdims=True)
        e_ref[slot, sidx] = jnp.exp(dm - rm)
        rows_ref[slot, sidx, 0] = jnp.broadcast_to(rm, (L, dk))
        rows_ref[slot, sidx, 1] = jnp.broadcast_to(gc, (L, dk))
        rows_ref[slot, sidx, 2] = jnp.broadcast_to(gc[last:last + 1, :] - gc + gcols[:, ci:ci + 1], (L, dk))


def _mlstm_stage2(qk_ref, v_ref, o_ref, c_ref, n_ref, m_ref, e_ref, rows_ref, slot, *, d, L, nh, dk, dv):
    last = L - 1 if d == 0 else 0
    rep = dv // dk
    wide = lambda x: jnp.concatenate([x] * rep, axis=1)
    ones = jnp.ones((L, dk), BF16)
    qk = qk_ref[0]
    v_all = v_ref[0]
    for h in range(nh):
        sidx = d * nh + h
        rm = rows_ref[slot, sidx, 0]
        gc = rows_ref[slot, sidx, 1]
        mprev = m_ref[sidx][0:1, :]
        inter = gc + mprev
        mt = jnp.maximum(rm, inter)
        corr = jnp.exp(rm - mt)
        sc = jnp.exp(inter - mt)
        q = qk[:, h * dk:(h + 1) * dk]
        k = qk[:, (nh + h) * dk:(nh + h + 1) * dk]
        v = v_all[:, h * dv:(h + 1) * dv]
        p = (e_ref[slot, sidx] * _dot_nt(q, k)).astype(BF16)
        cst = c_ref[sidx]
        nst = n_ref[sidx]
        num = wide(corr) * _dot(p, v) + wide(sc) * _dot(q, cst.astype(BF16))
        qn = _dot_nt(q, jnp.concatenate([nst] * (dk // 8), axis=0).astype(BF16))
        den = corr * _dot(p, ones) + sc * qn
        o_ref[0, :, h * dv:(h + 1) * dv] = num / wide(jnp.maximum(jnp.abs(den), jnp.exp(-mt)))
        mnew = mt[last:last + 1, :]
        wk = jnp.exp(rows_ref[slot, sidx, 2] - mnew)
        decay = jnp.exp(gc[last:last + 1, :] + mprev - mnew)
        kw = k.astype(F32) * wk
        c_ref[sidx] = wide(decay) * cst + _dot_tn(kw.astype(BF16), v)
        n_ref[sidx] = jnp.broadcast_to(decay * nst[0:1, :] + jnp.sum(kw, axis=0, keepdims=True), (8, dk))
        m_ref[sidx] = jnp.broadcast_to(mnew, (8, dk))


def _mlstm_body(g_f, gt_f, g_b, gt_b, qk_f, v_f, qk_b, v_b, of_ref, ob_ref, c_ref, n_ref, m_ref,
                e_ref, rows_ref, *, L, nh, dk, dv):
    j = pl.program_id(1)

    @pl.when(j == 0)
    def _():
        e_ref[1] = jnp.zeros(e_ref.shape[1:], F32)
        rows_ref[1] = jnp.zeros(rows_ref.shape[1:], F32)

    @pl.when(j <= 1)
    def _():
        c_ref[...] = jnp.zeros_like(c_ref)
        n_ref[...] = jnp.zeros_like(n_ref)
        m_ref[...] = jnp.zeros_like(m_ref)

    def step(cur):
        for d, (g, gt) in enumerate(((g_f, gt_f), (g_b, gt_b))):
            _mlstm_stage1(g, gt, e_ref, rows_ref, cur, d=d, L=L, nh=nh, dk=dk)
        for d, (qk, v, o) in enumerate(((qk_f, v_f, of_ref), (qk_b, v_b, ob_ref))):
            _mlstm_stage2(qk, v, o, c_ref, n_ref, m_ref, e_ref, rows_ref, 1 - cur,
                          d=d, L=L, nh=nh, dk=dk, dv=dv)

    for parity in (0, 1):
        pl.when(j % 2 == parity)(functools.partial(step, parity))


def _mlstm(qk, proj3, v_blk, gates, gates_t, nh, dk, dv):
    B, S, _ = qk.shape
    L = MLSTM_CHUNK
    nc = S // L
    G = gates.shape[-1]
    assert dv % dk == 0 and dk % 8 == 0
    prep_f = lambda b, j: (b, jnp.minimum(j, nc - 1))
    prep_b = lambda b, j: (b, jnp.maximum(nc - 1 - j, 0))
    run_f = lambda b, j: (b, jnp.maximum(j - 1, 0))
    run_b = lambda b, j: (b, jnp.minimum(nc - j, nc - 1))
    specs = []
    for im in (prep_f, prep_b):
        specs += [pl.BlockSpec((1, L, G), (lambda b, j, im=im: im(b, j) + (0,))),
                  pl.BlockSpec((1, G, L), (lambda b, j, im=im: (im(b, j)[0], 0, im(b, j)[1])))]
    for im in (run_f, run_b):
        specs += [pl.BlockSpec((1, L, 2 * nh * dk), (lambda b, j, im=im: im(b, j) + (0,))),
                  pl.BlockSpec((1, L, nh * dv), (lambda b, j, im=im: im(b, j) + (v_blk,)))]
    return pl.pallas_call(
        functools.partial(_mlstm_body, L=L, nh=nh, dk=dk, dv=dv),
        grid=(B, nc + 1),
        in_specs=specs,
        out_specs=[pl.BlockSpec((1, L, nh * dv), lambda b, j: run_f(b, j) + (0,)),
                   pl.BlockSpec((1, L, nh * dv), lambda b, j: run_b(b, j) + (0,))],
        out_shape=[jax.ShapeDtypeStruct((B, S, nh * dv), F32)] * 2,
        scratch_shapes=[pltpu.VMEM((2 * nh, dk, dv), F32),
                        pltpu.VMEM((2 * nh, 8, dk), F32),
                        pltpu.VMEM((2 * nh, 8, dk), F32),
                        pltpu.VMEM((2, 2 * nh, L, L), F32),
                        pltpu.VMEM((2, 2 * nh, 3, L, dk), F32)],
        compiler_params=_cparams(("arbitrary", "arbitrary")),
        name="mlstm_scan",
    )(gates, gates_t, gates, gates_t, qk, proj3, qk, proj3)


HGRN_TOP = 16


def _hgrn_levels(L):
    c = HGRN_TOP // 2
    out = []
    while c >= 1:
        out.append(c)
        c //= 2
    return out


def _hgrn_sum_matrix(L, d):
    t = np.arange(L)[:, None]
    r = np.arange(L)[None, :]
    if d == 0:
        blocks = [(r <= t), (r > t)]
    else:
        blocks = [(r >= t), (r < t)]
    for c in _hgrn_levels(L):
        base = (t // (2 * c)) * (2 * c)
        mid = base + c
        second = (t - base) >= c
        if d == 0:
            m = np.where(second, (r >= mid) & (r <= t), (r > t) & (r < mid))
        else:
            m = np.where(second, (r >= mid) & (r < t), (r >= t) & (r < mid))
        blocks.append(m)
    p = np.concatenate(blocks, axis=0).astype(np.float32)
    return np.concatenate([p, p, p], axis=1)


def _hgrn_prep(f_ref, lb_ref, p_ref, kb_ref, sums_ref, *, d, layer):
    slots = [lb_ref[k][d:d + 1, :] for k in range(lb_ref.shape[0])]
    top = functools.reduce(jnp.maximum, slots)
    es = [jnp.exp(s - top) for s in slots]
    lb = sum(es[:layer + 1]) / sum(es)
    f = lb + (1.0 - lb) * _sigmoid(f_ref[0])
    lgf = jnp.log2(f)
    kb_ref[d] = 1.0 - f
    hi = lgf.astype(BF16)
    r1 = lgf - hi.astype(F32)
    mid = r1.astype(BF16)
    lo = (r1 - mid.astype(F32)).astype(BF16)
    sums_ref[d] = _dot(p_ref[...], jnp.concatenate([hi, mid, lo], axis=0))


def _hgrn_head(q_ref, i_ref, o_ref, st_ref, kb_ref, sums_ref, h, *, d, L, nh, dk, dv):
    levels = _hgrn_levels(L)
    rowi = lax.broadcasted_iota(I32, (L, 1), 0)
    xor = lax.broadcasted_iota(I32, (L, L), 0) ^ lax.broadcasted_iota(I32, (L, L), 1)
    last = L - 1 if d == 0 else 0
    ks = slice(h * dk, (h + 1) * dk)
    vs = slice(h * dv, (h + 1) * dv)
    qh = q_ref[0, :, ks].astype(F32)
    kbh = kb_ref[d, :, ks]
    ih = i_ref[0, :, vs]
    bh = sums_ref[d, 0:L, ks]
    eb = jnp.exp2(bh)
    ea = jnp.exp2(sums_ref[d, L:2 * L, ks])
    top, slabs = HGRN_TOP, []
    for blk in range(L // top):
        rows = slice(blk * top, (blk + 1) * top)
        keys = slice(0, blk * top) if d == 0 else slice((blk + 1) * top, L)
        nkeys = keys.stop - keys.start
        if nkeys == 0:
            slabs.append(jnp.zeros((top, L), F32))
            continue
        edge = blk * top - 1 if d == 0 else (blk + 1) * top
        ref = bh[edge:edge + 1, :]
        zq = (qh[rows] * jnp.exp2(bh[rows] - ref)).astype(BF16)
        zk = (kbh[keys] * jnp.exp2(ref - bh[keys])).astype(BF16)
        g = _dot_nt(zq, zk)
        pad = jnp.zeros((top, L - nkeys), F32)
        slabs.append(jnp.concatenate([g, pad] if d == 0 else [pad, g], axis=1))
    a = jnp.concatenate(slabs, axis=0)
    for li, c in enumerate(levels):
        k = c.bit_length() - 1
        bit = (rowi >> k) & 1
        if d == 1:
            bit = 1 - bit
        el = jnp.exp2(sums_ref[d, (2 + li) * L:(3 + li) * L, ks])
        z = (jnp.where(bit == 1, qh, kbh) * el).astype(BF16)
        pair = ((xor >> k) + ((1 - bit) << 8)) == 1
        a = jnp.where(pair, _dot_nt(z, z), a)
    st = st_ref[d * nh + h]
    o = (_dot(a.astype(BF16), ih) + _dot_nt((qh * eb).astype(BF16), st.astype(BF16))
         + jnp.sum(qh * kbh, axis=1, keepdims=True) * ih.astype(F32))
    o_ref[0, :, vs] = o
    st_ref[d * nh + h] = st * eb[last:last + 1, :] + _dot_tn(ih, (kbh * ea).astype(BF16))


def _hgrn_body(q_f, i_f, f_f, q_b, i_b, f_b, lb_ref, pf_ref, pb_ref, of_ref, ob_ref, st_ref,
               kb_ref, sums_ref, *, L, nh, dk, dv, layer):
    @pl.when(pl.program_id(1) == 0)
    def _():
        st_ref[...] = jnp.zeros_like(st_ref)

    _hgrn_prep(f_f, lb_ref, pf_ref, kb_ref, sums_ref, d=0, layer=layer)
    _hgrn_prep(f_b, lb_ref, pb_ref, kb_ref, sums_ref, d=1, layer=layer)
    kw = dict(L=L, nh=nh, dk=dk, dv=dv)

    for h in range(nh):
        _hgrn_head(q_f, i_f, of_ref, st_ref, kb_ref, sums_ref, h, d=0, **kw)
        _hgrn_head(q_b, i_b, ob_ref, st_ref, kb_ref, sums_ref, h, d=1, **kw)


def _hgrn(proj3, q_blk, i_blk, fpre, lb, layer, nh, dk, dv):
    B, S, _ = proj3.shape
    L = HGRN_CHUNK
    nc = S // L
    W = nh * dk
    pf = jnp.asarray(_hgrn_sum_matrix(L, 0), BF16)
    pb = jnp.asarray(_hgrn_sum_matrix(L, 1), BF16)
    fw = lambda b, j: (b, j)
    bw = lambda b, j: (b, nc - 1 - j)
    specs = []
    for d, im in enumerate((fw, bw)):
        specs += [pl.BlockSpec((1, L, W), (lambda b, j, im=im: im(b, j) + (q_blk,))),
                  pl.BlockSpec((1, L, nh * dv), (lambda b, j, im=im: im(b, j) + (i_blk,))),
                  pl.BlockSpec((1, L, W), (lambda b, j, im=im, d=d: im(b, j) + (d,)))]
    specs += [pl.BlockSpec(lb.shape, lambda b, j: (0, 0, 0)),
              pl.BlockSpec(pf.shape, lambda b, j: (0, 0)),
              pl.BlockSpec(pb.shape, lambda b, j: (0, 0))]
    return pl.pallas_call(
        functools.partial(_hgrn_body, L=L, nh=nh, dk=dk, dv=dv, layer=layer),
        grid=(B, nc),
        in_specs=specs,
        out_specs=[pl.BlockSpec((1, L, nh * dv), lambda b, j: (b, j, 0)),
                   pl.BlockSpec((1, L, nh * dv), lambda b, j: (b, nc - 1 - j, 0))],
        out_shape=[jax.ShapeDtypeStruct((B, S, nh * dv), F32)] * 2,
        scratch_shapes=[pltpu.VMEM((2 * nh, dv, dk), F32),
                        pltpu.VMEM((2, L, W), F32),
                        pltpu.VMEM((2, pf.shape[0], W), F32)],
        compiler_params=_cparams(("arbitrary", "arbitrary")),
        name="hgrn2_scan",
    )(proj3, proj3, fpre, proj3, proj3, fpre, lb, pf, pb)


def _head_rms(x, nh, dh):
    outs = []
    for h in range(nh):
        xh = x[:, h * dh:(h + 1) * dh]
        outs.append(xh * lax.rsqrt(jnp.mean(xh * xh, axis=-1, keepdims=True) + LN_EPS))
    return jnp.concatenate(outs, axis=1)


def _post_body(hmf, hmb, hof, hob, om, gh, gm, ghh, x_ref, l0g, l0b, mg, hg, wbm, wbh, wout, l1g, l1b, wr,
               h1_ref, aff_ref, *, nh_m, dv_m, nh_h, dv_h, alpha):
    hm = _head_rms(hmf[...] + hmb[...], nh_m, dv_m) * mg[...] * _gate_sigmoid(om[...].astype(F32))
    hgate = gh[...].astype(F32)
    ho = _head_rms(hof[...] + hob[...], nh_h, dv_h) * hg[...] * (hgate * _gate_sigmoid(hgate))
    y_m = _dot(hm.astype(BF16), wbm[...])
    y_h = _dot(ho.astype(BF16), wbh[...])
    merged = _gate_sigmoid(gm[...].astype(F32)) * y_m + _gate_sigmoid(ghh[...].astype(F32)) * y_h
    mix = _dot(merged.astype(BF16), wout[...])
    h = _layer_norm(x_ref[...], l0g[...], l0b[...])
    h1 = _layer_norm(alpha * h + mix, l1g[...], l1b[...])
    h1_ref[...] = h1
    logits = _dot_nt(wr[...], h1.astype(BF16))
    z = jnp.exp(logits - jnp.max(logits, axis=0, keepdims=True))
    aff_ref[0] = z / jnp.sum(z, axis=0, keepdims=True)


def _post_mixer(hm_f, hm_b, ho_f, ho_b, proj, cols, x2, l0g, l0b, mg, hg, wbm, wbh, wout, l1g, l1b, wr_t,
                B, S, dims, alpha, tm=256):
    T, D = x2.shape
    nh_m, dv_m, nh_h, dv_h = dims
    vm, vh = nh_m * dv_m, nh_h * dv_h
    E = wr_t.shape[0]
    nbs = S // tm
    row = lambda i: (i, 0)
    const = lambda i: (0, 0)
    in_specs = [pl.BlockSpec((tm, vm), row), pl.BlockSpec((tm, vm), row),
                pl.BlockSpec((tm, vh), row), pl.BlockSpec((tm, vh), row),
                pl.BlockSpec((tm, vm), lambda i: (i, cols["o_m"] // vm)),
                pl.BlockSpec((tm, vh), lambda i: (i, cols["g_h"] // vh)),
                pl.BlockSpec((tm, D), lambda i: (i, cols["gate_m"] // D)),
                pl.BlockSpec((tm, D), lambda i: (i, cols["gate_h"] // D)),
                pl.BlockSpec((tm, D), row), pl.BlockSpec((1, D), const), pl.BlockSpec((1, D), const),
                pl.BlockSpec((1, vm), const), pl.BlockSpec((1, vh), const),
                pl.BlockSpec((vm, D), const), pl.BlockSpec((vh, D), const), pl.BlockSpec((D, D), const),
                pl.BlockSpec((1, D), const), pl.BlockSpec((1, D), const),
                pl.BlockSpec((E, D), const)]
    return pl.pallas_call(
        functools.partial(_post_body, nh_m=nh_m, dv_m=dv_m, nh_h=nh_h, dv_h=dv_h, alpha=alpha),
        grid=(T // tm,),
        in_specs=in_specs,
        out_specs=[pl.BlockSpec((tm, D), row),
                   pl.BlockSpec((1, E, tm), lambda i: (i // nbs, 0, i % nbs))],
        out_shape=[jax.ShapeDtypeStruct((T, D), F32), jax.ShapeDtypeStruct((B, E, S), F32)],
        compiler_params=_cparams(("arbitrary",)),
        name="mixer_out_ln1_router",
    )(hm_f, hm_b, ho_f, ho_b, proj, proj, proj, proj, x2, l0g, l0b, mg, hg, wbm, wbh, wout, l1g, l1b, wr_t)


def _prefix_counts(mask2, upper, ones, bdl):
    mb = mask2.astype(BF16)
    within = _dot(mb, upper)
    rowtot = _dot(mb, ones)
    before = _dot(bdl, rowtot.astype(BF16))
    return within + before, within


def _topk_body(a_ref, upper_ref, ones_ref, bdl_ref, idx_ref, pos_ref, off_ref, *, E, R, cap):
    a3 = a_ref[0]

    def count(m):
        return jnp.sum(jnp.sum(jnp.where(m, 1.0, 0.0), axis=2, keepdims=True), axis=1, keepdims=True)

    def as_float(bits):
        return pltpu.bitcast(jnp.broadcast_to(bits, a3.shape), F32)

    def step(it, cur):
        cand = cur | (jnp.int32(1) << (30 - it))
        return jnp.where(count(a3 >= as_float(cand)) >= cap, cand, cur)

    thr = lax.fori_loop(0, 31, step, jnp.zeros((E, 1, 1), I32))
    gt = a3 >= as_float(thr + 1)
    eq = jnp.logical_and(a3 >= as_float(thr), jnp.logical_not(gt))
    need = cap - count(gt)
    upper, ones, bdl = upper_ref[...], ones_ref[...], bdl_ref[...]
    eq_incl, _ = _prefix_counts(jnp.where(eq, 1.0, 0.0).reshape(E * R, 128), upper, ones, bdl)
    sel = jnp.logical_or(gt, jnp.logical_and(eq, eq_incl.reshape(E, R, 128) <= need))
    self32 = jnp.where(sel, 1.0, 0.0).reshape(E * R, 128)
    incl, within = _prefix_counts(self32, upper, ones, bdl)
    pos_ref[0] = jnp.where(sel, incl.reshape(E, R, 128) - 1.0, -1.0).astype(I32)

    slot = lax.broadcasted_iota(I32, (1, cap), 1).astype(F32)
    rsub = lax.broadcasted_iota(I32, (R, cap), 0).astype(F32)
    for e in range(E):
        inc_e = incl[e * R:(e + 1) * R]
        row_incl = inc_e[:, 127:128]
        row_excl = row_incl - within[e * R:(e + 1) * R][:, 127:128]
        off_ref[0, e] = jnp.broadcast_to(row_excl, (R, 128)).astype(I32)
        ridx = jnp.sum(jnp.where(row_incl <= slot, 1.0, 0.0), axis=0, keepdims=True)
        onehot = rsub == ridx
        start = jnp.sum(jnp.where(onehot, row_excl, 0.0), axis=0, keepdims=True)
        local = slot - start
        pg = _dot_tn(within[e * R:(e + 1) * R].astype(BF16), jnp.where(onehot, 1.0, 0.0).astype(BF16))
        lane = jnp.sum(jnp.where(pg <= local, 1.0, 0.0), axis=0, keepdims=True)
        idx_ref[0, e:e + 1, :] = (ridx * 128.0 + lane).astype(I32)


def _topk(aff_t, cap):
    B, E, S = aff_t.shape
    R = S // 128
    k = np.arange(128)
    upper = jnp.asarray((k[:, None] <= k[None, :]).astype(np.float32), BF16)
    ones = jnp.ones((128, 128), BF16)
    r = np.arange(E * R)
    bdl = jnp.asarray(((r[:, None] // R == r[None, :] // R) & (r[None, :] < r[:, None])).astype(np.float32), BF16)
    const2 = lambda b: (0, 0)
    return pl.pallas_call(
        functools.partial(_topk_body, E=E, R=R, cap=cap),
        grid=(B,),
        in_specs=[pl.BlockSpec((1, E, R, 128), lambda b: (b, 0, 0, 0)),
                  pl.BlockSpec((128, 128), const2), pl.BlockSpec((128, 128), const2),
                  pl.BlockSpec((E * R, E * R), const2)],
        out_specs=[pl.BlockSpec((1, E, cap), lambda b: (b, 0, 0)),
                   pl.BlockSpec((1, E, R, 128), lambda b: (b, 0, 0, 0)),
                   pl.BlockSpec((1, E, R, 128), lambda b: (b, 0, 0, 0))],
        out_shape=[jax.ShapeDtypeStruct((B, E, cap), I32),
                   jax.ShapeDtypeStruct((B, E, R, 128), I32),
                   jax.ShapeDtypeStruct((B, E, R, 128), I32)],
        compiler_params=_cparams(("arbitrary",)),
        name="expert_topk",
    )(aff_t.reshape(B, E, R, 128), upper, ones, bdl)


def _ffn_body(idx_ref, h_hbm, wg_ref, wu_ref, wd_ref, ye_ref, xg, xb, acc, sem, *, E, B, C, rps):
    e = pl.program_id(0)
    b = pl.program_id(1)
    f = pl.program_id(2)
    nf = pl.num_programs(2)
    nrows = xg.shape[1]
    pair = e * B + b
    slot = pair % 2
    nxt = (pair + 1) % (E * B)
    e_n = nxt // B
    b_n = nxt % B

    def row_copy(bb, src_row, sl, dst_row):
        return pltpu.make_async_copy(h_hbm.at[bb, pl.ds(src_row, 1), :],
                                     xg.at[sl, pl.ds(dst_row, 1), :], sem.at[sl])

    def src_row(bb, ee, s):
        return idx_ref[(bb * E + ee) * C + jnp.minimum(s, C - 1)]

    def wait_all(sl):
        def wait(s, carry):
            row_copy(0, 0, sl, s).wait()
            return carry
        lax.fori_loop(0, nrows, wait, 0, unroll=rps)

    @pl.when(f == 0)
    def _():
        @pl.when(pair == 0)
        def _():
            def issue(s, carry):
                row_copy(b, src_row(b, e, s), slot, s).start()
                return carry
            lax.fori_loop(0, nrows, issue, 0)

        wait_all(slot)
        xb[...] = xg[slot, 0:C, :].astype(BF16)
        acc[...] = jnp.zeros_like(acc)

    for r in range(rps):
        s = f * rps + r
        row_copy(b_n, src_row(b_n, e_n, s), 1 - slot, s).start()

    x = xb[...]
    g = _dot(x, wg_ref[0].astype(BF16))
    u = _dot(x, wu_ref[0].astype(BF16))
    hid = (g * _gate_sigmoid(g) * u).astype(BF16)
    acc[...] += _dot(hid, wd_ref[0].astype(BF16))

    @pl.when(f == nf - 1)
    def _():
        ye_ref[0, 0] = acc[...].astype(ye_ref.dtype)

        @pl.when(pair == E * B - 1)
        def _():
            wait_all(1 - slot)


def _expert_ffn(idx, h1, w_gate, w_up, w_down, tf=256):
    B, S, D = h1.shape
    E, _, F = w_gate.shape
    C = idx.shape[-1]
    nf = F // tf
    rps = pl.cdiv(C, nf)
    grid_spec = pltpu.PrefetchScalarGridSpec(
        num_scalar_prefetch=1,
        grid=(E, B, nf),
        in_specs=[pl.BlockSpec(memory_space=pl.ANY),
                  pl.BlockSpec((1, D, tf), lambda e, b, f, idx: (e, 0, f)),
                  pl.BlockSpec((1, D, tf), lambda e, b, f, idx: (e, 0, f)),
                  pl.BlockSpec((1, tf, D), lambda e, b, f, idx: (e, f, 0))],
        out_specs=pl.BlockSpec((1, 1, C, D), lambda e, b, f, idx: (b, e, 0, 0)),
        scratch_shapes=[pltpu.VMEM((2, nf * rps, D), F32), pltpu.VMEM((C, D), BF16),
                        pltpu.VMEM((C, D), F32), pltpu.SemaphoreType.DMA((2,))],
    )
    return pl.pallas_call(
        functools.partial(_ffn_body, E=E, B=B, C=C, rps=rps),
        grid_spec=grid_spec,
        out_shape=jax.ShapeDtypeStruct((B, E, C, D), BF16),
        compiler_params=_cparams(("arbitrary", "arbitrary", "arbitrary")),
        name="expert_ffn",
    )(idx.reshape(-1), h1, w_gate, w_up, w_down)


COMBINE_ROWS = 16
COMBINE_KT = 256


def _combine_body(st_ref, h_ref, pos_ref, aff_ref, ye_hbm, g_ref, b_ref, o_ref, buf, acc, sem,
                  *, E, B, TB, NTB, alpha):
    b = pl.program_id(0)
    tb = pl.program_id(1)
    rows, kt = COMBINE_ROWS, COMBINE_KT
    step = b * NTB + tb
    slot = step % 2

    def chunk_copy(bb, e, src, sl, dst):
        return pltpu.make_async_copy(ye_hbm.at[bb, e, pl.ds(src, rows), :],
                                     buf.at[sl, pl.ds(dst, rows), :], sem.at[sl])

    def plan(bb, tt):
        koff = jnp.int32(0)
        out = []
        for e in range(E):
            base = (bb * E + e) * (NTB + 1) + tt
            ws = (st_ref[base] // rows) * rows
            n = (st_ref[base + 1] - ws + rows - 1) // rows
            out.append((ws, n, koff))
            koff = koff + n * rows
        return out, koff

    def issue_all(bb, tt, sl):
        for e, (ws, n, koff) in enumerate(plan(bb, tt)[0]):
            def issue(c, carry, e=e, ws=ws, koff=koff):
                chunk_copy(bb, e, pl.multiple_of(ws + c * rows, rows), sl,
                           pl.multiple_of(koff + c * rows, rows)).start()
                return carry
            lax.fori_loop(0, n, issue, 0)

    @pl.when(step == 0)
    def _():
        buf[...] = jnp.zeros_like(buf)
        issue_all(b, tb, slot)

    cur, ktot = plan(b, tb)

    def wait(c, carry):
        chunk_copy(0, 0, 0, slot, 0).wait()
        return carry

    lax.fori_loop(0, ktot // rows, wait, 0)

    @pl.when(step + 1 < B * NTB)
    def _():
        issue_all((step + 1) // NTB, (step + 1) % NTB, 1 - slot)

    pos = pos_ref[0]
    aff = aff_ref[0]
    keys = []
    for e, (ws, n, koff) in enumerate(cur):
        pe = pos[e:e + 1, :]
        keys.append(jnp.where(pe >= 0, pe + (koff - ws), -1))

    def onehot_t(kbase):
        ksub = lax.broadcasted_iota(I32, (kt, TB), 0) + kbase
        o = jnp.zeros((kt, TB), F32)
        for e in range(E):
            o = jnp.where(ksub == keys[e], aff[e:e + 1, :], o)
        return o.astype(BF16)

    acc[...] = _dot_tn(onehot_t(0), buf[slot, 0:kt, :])

    def kstep(kc, carry):
        kbase = pl.multiple_of(kc * kt, kt)
        acc[...] += _dot_tn(onehot_t(kbase), buf[slot, pl.ds(kbase, kt), :])
        return carry

    lax.fori_loop(1, (ktot + kt - 1) // kt, kstep, 0)
    o_ref[0] = _layer_norm(alpha * h_ref[0] + acc[...], g_ref[...], b_ref[...])


def _combine(starts, h1, pos, aff, ye, g, bb, alpha, TB):
    B, S, D = h1.shape
    E = ye.shape[1]
    NTB = S // TB
    kmax = pl.cdiv(E * (TB + 2 * COMBINE_ROWS), COMBINE_KT) * COMBINE_KT
    grid_spec = pltpu.PrefetchScalarGridSpec(
        num_scalar_prefetch=1,
        grid=(B, NTB),
        in_specs=[pl.BlockSpec((1, TB, D), lambda b, t, st: (b, t, 0)),
                  pl.BlockSpec((1, E, TB), lambda b, t, st: (b, 0, t)),
                  pl.BlockSpec((1, E, TB), lambda b, t, st: (b, 0, t)),
                  pl.BlockSpec(memory_space=pl.ANY),
                  pl.BlockSpec((1, D), lambda b, t, st: (0, 0)),
                  pl.BlockSpec((1, D), lambda b, t, st: (0, 0))],
        out_specs=pl.BlockSpec((1, TB, D), lambda b, t, st: (b, t, 0)),
        scratch_shapes=[pltpu.VMEM((2, kmax, D), BF16), pltpu.VMEM((TB, D), F32),
                        pltpu.SemaphoreType.DMA((2,))],
    )
    return pl.pallas_call(
        functools.partial(_combine_body, E=E, B=B, TB=TB, NTB=NTB, alpha=alpha),
        grid_spec=grid_spec,
        out_shape=jax.ShapeDtypeStruct((B, S, D), F32),
        compiler_params=_cparams(("arbitrary", "arbitrary")),
        name="combine_ln2",
    )(starts.reshape(-1), h1, pos, aff, ye, g.reshape(1, D), bb.reshape(1, D))


def _layer(x2, l0g, l0b, B, S, lb, layer, w_in, b_in, conv_w, mg, hg, w_bm, w_bh, w_out, l1g, l1b,
           w_router, w_gate, w_up, w_down, l2g, l2b, alpha):
    T, D = x2.shape
    nh_m, dv_m = mg.shape
    nh_h, dv_h = hg.shape
    ksz, qk2 = conv_w.shape
    dk_m = qk2 // (2 * nh_m)
    v_m = nh_m * dv_m
    v_h = nh_h * dv_h
    q_h = lb.shape[-1]
    dk_h = q_h // nh_h
    E = w_router.shape[1]
    ng = 4 * nh_m
    sizes = (qk2, v_m, v_m, ng // 2, ng // 2, q_h, 2 * q_h, v_h, v_h, D, D)
    names = ("qk_m", "v_m", "o_m", "i_m", "f_m", "q_h", "f_h", "i_h", "g_h", "gate_m", "gate_h")
    start = dict(zip(names, np.concatenate([[0], np.cumsum(sizes)[:-1]]).tolist()))
    size = dict(zip(names, sizes))
    sl = lambda a, n: a[..., start[n]:start[n] + size[n]]
    a_names = ("qk_m", "v_m", "o_m", "q_h", "i_h", "g_h", "gate_m", "gate_h")
    w_a = jnp.concatenate([sl(w_in, n) for n in a_names], axis=1).astype(BF16)
    b_a = jnp.concatenate([sl(b_in, n) for n in a_names])
    cols = dict(zip(a_names, np.concatenate([[0], np.cumsum([size[n] for n in a_names])[:-1]]).tolist()))
    gpad = 128 - ng
    g_names = ("f_h", "i_m", "f_m")
    w_fg = jnp.pad(jnp.concatenate([sl(w_in, n) for n in g_names], axis=1), ((0, 0), (0, gpad))).astype(BF16)
    b_fg = jnp.pad(jnp.concatenate([sl(b_in, n) for n in g_names]), (0, gpad))

    hb = _ln_in(x2, l0g, l0b)
    proj = _proj(hb, w_a, b_a, BF16, tm=1024, tn=1024)
    fg = _proj(hb, w_fg, b_fg, F32, tm=1024, tn=w_fg.shape[1])
    gates = fg[:, 2 * q_h:2 * q_h + ng].reshape(B, S, ng)
    proj3 = proj.reshape(B, S, -1)

    kscale = jnp.concatenate([jnp.ones((qk2 // 2,), F32), jnp.full((qk2 // 2,), dk_m ** -0.5, F32)])
    qk = _qk_conv(proj3, conv_w, kscale.reshape(1, qk2), qk2)
    hm_f, hm_b = _mlstm(qk, proj3, cols["v_m"] // v_m, gates, jnp.swapaxes(gates, 1, 2), nh_m, dk_m, dv_m)

    ho_f, ho_b = _hgrn(proj3, cols["q_h"] // q_h, cols["i_h"] // v_h, fg.reshape(B, S, -1), lb,
                       layer, nh_h, dk_h, dv_h)

    h1, aff_t = _post_mixer(hm_f.reshape(T, v_m), hm_b.reshape(T, v_m), ho_f.reshape(T, v_h),
                            ho_b.reshape(T, v_h), proj, cols, x2, l0g, l0b, mg.reshape(1, v_m), hg.reshape(1, v_h),
                            w_bm.astype(BF16), w_bh.astype(BF16), w_out.astype(BF16),
                            l1g.reshape(1, D), l1b.reshape(1, D), w_router.T.astype(BF16),
                            B, S, (nh_m, dv_m, nh_h, dv_h), alpha)

    cap = CAP_FACTOR * S // E
    idx, pos, off = _topk(aff_t, cap)
    h1_3 = h1.reshape(B, S, D)
    ye = _expert_ffn(idx, h1_3, w_gate, w_up, w_down)
    TB = 256
    starts = jnp.concatenate([off[:, :, ::TB // 128, 0], jnp.full((B, E, 1), cap, I32)], axis=2)
    return _combine(starts, h1_3, pos.reshape(B, E, S), aff_t, ye, l2g, l2b, alpha, TB=TB).reshape(T, D)


def kernel(x, ln_in_g, ln_in_b, hgrn_lb_logits, w_in, b_in, conv_w, mlstm_norm_g, hgrn_norm_g,
           w_branch_m, w_branch_h, w_out, ln1_g, ln1_b, w_router, w_gate_e, w_up_e, w_down_e,
           ln2_g, ln2_b):
    B, S, D = x.shape
    depth = w_in.shape[0]
    alpha = (2.0 * depth) ** 0.25
    lb_logits = jnp.swapaxes(hgrn_lb_logits.astype(F32), 0, 1)
    assert depth == 1, "the layer-to-layer hand-over is written for a single layer"
    out = _layer(x.reshape(B * S, D), ln_in_g.reshape(1, D), ln_in_b.reshape(1, D), B, S, lb_logits, 0,
                 w_in[0], b_in[0], conv_w[0], mlstm_norm_g[0], hgrn_norm_g[0], w_branch_m[0],
                 w_branch_h[0], w_out[0], ln1_g[0], ln1_b[0], w_router[0], w_gate_e[0], w_up_e[0],
                 w_down_e[0], ln2_g[0], ln2_b[0], alpha)
    return out.reshape(B, S, D)
```
